```python
import math
import jax, jax.numpy as jnp
from jax import lax
import numpy as np

D_MODEL = 1024
BATCH = 32
SEQ = 2048
DEPTH = 4
DEC_BATCH = 4
DEC_SEQ = 4096
PAST_LEN = 128

MEM_LEN = 256
CONV_W = D_MODEL // 2
CONV_K = 31
DA_HEADS = 4
DA_HD = D_MODEL // 16
DA_WIDTH = DA_HEADS * 2 * DA_HD
ROT_DIM = DA_HD // 4
ROPE_THETA = 500000.0
QBLK = 128
HG_HEADS = 4
HG_DK = D_MODEL // 8
HG_DV = D_MODEL // 8
HG_WIDTH = HG_HEADS * HG_DK
CHUNK = 64
GATE_FLOOR = 1e-30
N_BRANCH = 3
IN_SIZES = (CONV_W, CONV_W, DA_WIDTH, DA_WIDTH, DA_WIDTH, HG_WIDTH, HG_WIDTH, HG_WIDTH,
            HG_HEADS * HG_DV, HG_HEADS * HG_DV, N_BRANCH * D_MODEL)
N_IN = sum(IN_SIZES)
X_HEADS = 4
X_HD = D_MODEL // X_HEADS
D_FF = 2816
FFN_K = 3
EPS = 1e-6

kernel_name = "hybrid_conv_diffattn_hgrn2_encoder"

F32 = jnp.float32


def rmsnorm(x, g):
    xf = x.astype(F32)
    y = xf * lax.rsqrt(jnp.mean(xf * xf, axis=-1, keepdims=True) + EPS)
    return (y * g.astype(F32)).astype(x.dtype)


def layernorm(x, g, b):
    xf = x.astype(F32)
    mu = jnp.mean(xf, axis=-1, keepdims=True)
    var = jnp.mean(jnp.square(xf - mu), axis=-1, keepdims=True)
    return ((xf - mu) * lax.rsqrt(var + EPS) * g.astype(F32) + b.astype(F32)).astype(x.dtype)


def dwconv(x, w, b):
    C = x.shape[-1]
    y = lax.conv_general_dilated(x, w[:, None, :].astype(x.dtype), window_strides=(1,), padding="SAME",
                                 dimension_numbers=("NWC", "WIO", "NWC"), feature_group_count=C)
    return y + b.astype(x.dtype)


def rope_tables(S, dtype):
    inv = 1.0 / (ROPE_THETA ** (jnp.arange(0, ROT_DIM, 2, dtype=F32) / ROT_DIM))
    ang = jnp.arange(S, dtype=F32)[:, None] * inv[None, :]
    return jnp.cos(ang).astype(dtype), jnp.sin(ang).astype(dtype)


def apply_partial_rope(x, cos, sin):
    half = ROT_DIM // 2
    c = cos[None, :, None, :]
    s = sin[None, :, None, :]
    x1 = x[..., :half]
    x2 = x[..., half:ROT_DIM]
    return jnp.concatenate([x1 * c - x2 * s, x2 * c + x1 * s, x[..., ROT_DIM:]], axis=-1)


def conv_branch(a, b, dw_w, dw_b, ln_g, ln_b, w_out):
    c = a * jax.nn.sigmoid(b)
    c = dwconv(c, dw_w, dw_b)
    c = jax.nn.silu(layernorm(c, ln_g, ln_b))
    return c @ w_out


def diff_attention_branch(aq, ak, av, lam_params, subln_g, w_out, layer):
    B, S, _ = aq.shape
    cos, sin = rope_tables(S, aq.dtype)
    q = apply_partial_rope(aq.reshape(B, S, DA_HEADS * 2, DA_HD), cos, sin).reshape(B, S, DA_HEADS, 2, DA_HD)
    k = apply_partial_rope(ak.reshape(B, S, DA_HEADS * 2, DA_HD), cos, sin).reshape(B, S, DA_HEADS, 2, DA_HD)
    v = av.reshape(B, S, DA_HEADS, 2 * DA_HD)
    lam_init = 0.8 - 0.6 * math.exp(-0.3 * layer)
    lp = lam_params.astype(F32)
    lam = jnp.exp(jnp.sum(lp[0] * lp[1])) - jnp.exp(jnp.sum(lp[2] * lp[3])) + lam_init
    scale = DA_HD ** -0.5
    nb = S // QBLK
    qb = q.reshape(B, nb, QBLK, DA_HEADS, 2, DA_HD).transpose(1, 0, 2, 3, 4, 5)

    def block(qi):
        s = jnp.einsum("bqhcd,bkhcd->bhcqk", qi, k).astype(F32) * scale
        p = jax.nn.softmax(s, axis=-1)
        w = p[:, :, 0] - lam * p[:, :, 1]
        return jnp.einsum("bhqk,bkhe->bqhe", w.astype(v.dtype), v)

    o = lax.map(block, qb)
    o = o.transpose(1, 0, 2, 3, 4).reshape(B, S, DA_HEADS, 2 * DA_HD)
    o = rmsnorm(o, subln_g) * (1.0 - lam_init)
    return o.reshape(B, S, DA_WIDTH) @ w_out


def hgrn_chunk_scan(q, k, v, g):
    B, S, H, K = q.shape
    V = v.shape[-1]
    n = S // CHUNK

    def to_chunks(t):
        return t.reshape(B, n, CHUNK, H, t.shape[-1]).transpose(1, 0, 3, 2, 4)

    mask = jnp.tril(jnp.ones((CHUNK, CHUNK), dtype=bool))[:, :, None]

    def step(state, inp):
        qc, kc, vc, gc = inp
        G = jnp.cumsum(gc, axis=2)
        inter = jnp.einsum("bhtk,bhkv->bhtv", qc * jnp.exp(G), state)
        rel = G[:, :, :, None, :] - G[:, :, None, :, :]
        decay = jnp.where(mask, jnp.exp(jnp.where(mask, rel, 0.0)), 0.0)
        A = jnp.einsum("bhtk,bhsk,bhtsk->bhts", qc, kc, decay)
        intra = jnp.einsum("bhts,bhsv->bhtv", A, vc)
        G_end = G[:, :, -1]
        new_state = jnp.exp(G_end)[..., None] * state + jnp.einsum(
            "bhsk,bhsv->bhkv", kc * jnp.exp(G_end[:, :, None, :] - G), vc)
        return new_state, inter + intra

    s0 = jnp.zeros((B, H, K, V), F32)
    _, o = lax.scan(step, s0, (to_chunks(q), to_chunks(k), to_chunks(v), to_chunks(g)))
    return o.transpose(1, 0, 3, 2, 4).reshape(B, S, H, V)


def hgrn2_branch(hq, hf_fwd, hf_bwd, hi, hgate, lb_dirs, norm_g, w_out):
    B, S, _ = hq.shape
    q = jax.nn.silu(hq.astype(F32)).reshape(B, S, HG_HEADS, HG_DK)
    v = hi.astype(F32).reshape(B, S, HG_HEADS, HG_DV)

    def gates(f_logit, lb):
        f = f_logit.astype(F32).reshape(B, S, HG_HEADS, HG_DK)
        lb = lb.reshape(HG_HEADS, HG_DK)
        forget = lb + (1.0 - lb) * jax.nn.sigmoid(f)
        g = jnp.log(jnp.maximum(forget, GATE_FLOOR))
        kk = 1.0 - forget
        return kk, g

    k_f, g_f = gates(hf_fwd, lb_dirs[0])
    k_b, g_b = gates(hf_bwd, lb_dirs[1])
    o_f = hgrn_chunk_scan(q, k_f, v, g_f)
    o_b = jnp.flip(hgrn_chunk_scan(jnp.flip(q, 1), jnp.flip(k_b, 1), jnp.flip(v, 1), jnp.flip(g_b, 1)), 1)
    o = rmsnorm(o_f + o_b, norm_g) * jax.nn.silu(hgate.astype(F32).reshape(B, S, HG_HEADS, HG_DV))
    return o.reshape(B, S, HG_WIDTH).astype(hq.dtype) @ w_out


def cross_attention(hn, mn, w_q, w_kv, w_o):
    B, S, _ = hn.shape
    M = mn.shape[1]
    q = (hn @ w_q).reshape(B, S, X_HEADS, X_HD)
    kv = mn @ w_kv
    k = kv[..., :D_MODEL].reshape(B, M, X_HEADS, X_HD)
    v = kv[..., D_MODEL:].reshape(B, M, X_HEADS, X_HD)
    s = jnp.einsum("bqhd,bkhd->bhqk", q, k).astype(F32) * (X_HD ** -0.5)
    p = jax.nn.softmax(s, axis=-1).astype(v.dtype)
    o = jnp.einsum("bhqk,bkhd->bqhd", p, v).reshape(B, S, D_MODEL)
    return o @ w_o


def conv_ffn(hn, w_up, dw_w, dw_b, w_down):
    u = dwconv(hn @ w_up, dw_w, dw_b)
    a, b = jnp.split(u, 2, axis=-1)
    return (jax.nn.silu(a) * b) @ w_down


def encoder_trunk(x, mem, p, lbs):
    B, S, _ = x.shape
    split_points = [int(s) for s in np.cumsum(IN_SIZES)[:-1]]
    for l in range(DEPTH):
        xn = rmsnorm(x, p["norm_mix_g"][l])
        proj = xn @ p["w_in"][l]
        ca, cb, aq, ak, av, hq, hff, hfb, hi, hg, gl = jnp.split(proj, split_points, axis=-1)
        y_c = conv_branch(ca, cb, p["conv_dw_w"][l], p["conv_dw_b"][l], p["conv_ln_g"][l],
                          p["conv_ln_b"][l], p["w_conv_out"][l])
        y_a = diff_attention_branch(aq, ak, av, p["attn_lambda"][l], p["attn_subln_g"][l],
                                    p["w_attn_out"][l], l)
        y_h = hgrn2_branch(hq, hff, hfb, hi, hg, lbs[l], p["hg_norm_g"][l], p["w_hg_out"][l])
        gt = jax.nn.sigmoid(gl.astype(F32)).astype(x.dtype).reshape(B, S, N_BRANCH, D_MODEL)
        merged = gt[:, :, 0] * y_c + gt[:, :, 1] * y_a + gt[:, :, 2] * y_h
        x = x + merged @ p["w_o"][l]
        hn = rmsnorm(x, p["norm_cross_g"][l])
        mn = rmsnorm(mem, p["norm_mem_g"][l])
        x = x + cross_attention(hn, mn, p["w_cq"][l], p["w_ckv"][l], p["w_co"][l])
        hn = rmsnorm(x, p["norm_ffn_g"][l])
        x = x + conv_ffn(hn, p["w_up"][l], p["ffn_dw_w"][l], p["ffn_dw_b"][l], p["w_down"][l])
    return rmsnorm(x, p["final_norm_g"])


def setup_inputs(seed: int = 0) -> dict:
    key = jax.random.key(seed)
    ks = jax.random.split(key, 40)
    it = iter(range(40))

    def nrm(shape, scale):
        return scale * jax.random.normal(ks[next(it)], shape, F32)

    def gain(shape):
        return 1.0 + nrm(shape, 0.02)

    L, D = DEPTH, D_MODEL
    return {
        "x_prompt": nrm((BATCH, SEQ, D), 1.0),
        "x_sample": nrm((DEC_BATCH, DEC_SEQ, D), 1.0),
        "mem_prompt": nrm((BATCH, MEM_LEN, D), 1.0),
        "mem_sample": nrm((DEC_BATCH, MEM_LEN, D), 1.0),
        "norm_mix_g": gain((L, D)),
        "w_in": nrm((L, D, N_IN), D ** -0.5),
        "conv_dw_w": nrm((L, CONV_K, CONV_W), CONV_K ** -0.5),
        "conv_dw_b": nrm((L, CONV_W), 0.02),
        "conv_ln_g": gain((L, CONV_W)),
        "conv_ln_b": nrm((L, CONV_W), 0.02),
        "w_conv_out": nrm((L, CONV_W, D), CONV_W ** -0.5),
        "attn_lambda": nrm((L, 4, DA_HD), 0.1),
        "attn_subln_g": gain((L, 2 * DA_HD)),
        "w_attn_out": nrm((L, DA_WIDTH, D), DA_WIDTH ** -0.5),
        "hg_lb_param": nrm((L, 2, HG_WIDTH), 0.1),
        "hg_norm_g": gain((L, HG_DV)),
        "w_hg_out": nrm((L, HG_WIDTH, D), HG_WIDTH ** -0.5),
        "w_o": nrm((L, D, D), D ** -0.5),
        "norm_cross_g": gain((L, D)),
        "norm_mem_g": gain((L, D)),
        "w_cq": nrm((L, D, D), D ** -0.5),
        "w_ckv": nrm((L, D, 2 * D), D ** -0.5),
        "w_co": nrm((L, D, D), D ** -0.5),
        "norm_ffn_g": gain((L, D)),
        "w_up": nrm((L, D, 2 * D_FF), D ** -0.5),
        "ffn_dw_w": nrm((L, FFN_K, 2 * D_FF), FFN_K ** -0.5),
        "ffn_dw_b": nrm((L, 2 * D_FF), 0.02),
        "w_down": nrm((L, D_FF, D), D_FF ** -0.5),
        "final_norm_g": gain((D,)),
    }


def reference(x_prompt, x_sample, mem_prompt, mem_sample, norm_mix_g, w_in, conv_dw_w, conv_dw_b,
              conv_ln_g, conv_ln_b, w_conv_out, attn_lambda, attn_subln_g, w_attn_out, hg_lb_param,
              hg_norm_g, w_hg_out, w_o, norm_cross_g, norm_mem_g, w_cq, w_ckv, w_co, norm_ffn_g,
              w_up, ffn_dw_w, ffn_dw_b, w_down, final_norm_g):
    p = {
        "norm_mix_g": norm_mix_g, "w_in": w_in, "conv_dw_w": conv_dw_w, "conv_dw_b": conv_dw_b,
        "conv_ln_g": conv_ln_g, "conv_ln_b": conv_ln_b, "w_conv_out": w_conv_out,
        "attn_lambda": attn_lambda, "attn_subln_g": attn_subln_g, "w_attn_out": w_attn_out,
        "hg_norm_g": hg_norm_g, "w_hg_out": w_hg_out, "w_o": w_o,
        "norm_cross_g": norm_cross_g, "norm_mem_g": norm_mem_g, "w_cq": w_cq, "w_ckv": w_ckv,
        "w_co": w_co, "norm_ffn_g": norm_ffn_g, "w_up": w_up, "ffn_dw_w": ffn_dw_w,
        "ffn_dw_b": ffn_dw_b, "w_down": w_down, "final_norm_g": final_norm_g,
    }
    lp = jax.nn.softmax(hg_lb_param.astype(F32), axis=0)
    lbs = jnp.cumsum(lp, axis=0) - lp[0:1]
    y_prompt = encoder_trunk(x_prompt, mem_prompt, p, lbs)
    y_sample = encoder_trunk(x_sample, mem_sample, p, lbs)
    return (y_prompt, y_sample)
```

```python
import functools
import math

import jax
import jax.numpy as jnp
from jax import lax
from jax.experimental import pallas as pl
from jax.experimental.pallas import tpu as pltpu

F32 = jnp.float32
BF16 = jnp.bfloat16

D_MODEL = 1024
DEPTH = 4
CONV_W = 512
CONV_K = 31
CONV_HALO = 16
DA_HEADS = 4
DA_HD = 64
ROT_DIM = 16
ROPE_THETA = 500000.0
HG_HEADS = 4
HG_DK = 128
HG_CHUNK = 64
HG_SUB = 8
GATE_FLOOR = 1e-30
X_HEADS = 4
X_HD = 256
D_FF = 2816
FF_CHUNK = 1408
FFN_HALO = 8
N_IN = 8192
EPS = 1e-6

COL_CA, COL_CB, COL_AQ, COL_AK, COL_AV, COL_HQ, COL_HFF, COL_HFB, COL_HI, COL_HG = range(10)
COL_GATE_1024 = 5

VMEM_LIMIT = 56 * 1024 * 1024


def _cparams(sem):
    return pltpu.CompilerParams(dimension_semantics=sem, vmem_limit_bytes=VMEM_LIMIT)


def _rms(x, g):
    ms = jnp.mean(x * x, axis=-1, keepdims=True)
    return x * lax.rsqrt(ms + EPS) * g


def _sigmoid(x):
    return 1.0 / (1.0 + jnp.exp(-x))


def _silu(x):
    return x * _sigmoid(x)


def _dot(a, b):
    return jnp.dot(a, b, preferred_element_type=F32)


def _dot_nt(a, b):
    return lax.dot_general(a, b, (((1,), (1,)), ((), ())), preferred_element_type=F32)


def _dot_tn(a, b):
    return lax.dot_general(a, b, (((0,), (0,)), ((), ())), preferred_element_type=F32)


def _inproj_kernel(x_ref, g_ref, w_ref, o_ref, xn_ref):
    @pl.when(pl.program_id(2) == 0)
    def _():
        xn_ref[...] = _rms(x_ref[...], g_ref[...]).astype(BF16)

    o_ref[...] = _dot(xn_ref[...], w_ref[...])


def _inproj(x, g, w, tm=512, tn=1024):
    B, S, D = x.shape
    N = w.shape[1]
    return pl.pallas_call(
        _inproj_kernel,
        grid=(B, S // tm, N // tn),
        in_specs=[
            pl.BlockSpec((None, tm, D), lambda b, i, j: (b, i, 0)),
            pl.BlockSpec((1, D), lambda b, i, j: (0, 0)),
            pl.BlockSpec((D, tn), lambda b, i, j: (0, j)),
        ],
        out_specs=pl.BlockSpec((None, tm, tn), lambda b, i, j: (b, i, j)),
        out_shape=jax.ShapeDtypeStruct((B, S, N), F32),
        scratch_shapes=[pltpu.VMEM((tm, D), BF16)],
        compiler_params=_cparams(("parallel", "parallel", "arbitrary")),
        name="inproj",
    )(x, g, w)


def _conv_kernel(ap_ref, a_ref, an_ref, bp_ref, b_ref, bn_ref, w_ref, bias_ref, lg_ref, lb_ref,
                 o_ref, ext_ref, *, ts, sub):
    i = pl.program_id(1)
    n = pl.num_programs(1)
    H = CONV_HALO

    def glu(a, b):
        return a * _sigmoid(b)

    ext_ref[0:H, :] = jnp.where(i > 0, glu(ap_ref[...], bp_ref[...]), 0.0)
    ext_ref[H:H + ts, :] = glu(a_ref[...], b_ref[...])
    ext_ref[H + ts:H + ts + H, :] = jnp.where(i < n - 1, glu(an_ref[...], bn_ref[...]), 0.0)

    off = H - CONV_K // 2
    for r in range(ts // sub):
        acc = jnp.zeros((sub, CONV_W), F32)
        for j in range(CONV_K):
            acc = acc + w_ref[j:j + 1, :] * ext_ref[pl.ds(r * sub + j + off, sub), :]
        acc = acc + bias_ref[...]
        mu = jnp.mean(acc, axis=-1, keepdims=True)
        d = acc - mu
        var = jnp.mean(d * d, axis=-1, keepdims=True)
        y = d * lax.rsqrt(var + EPS) * lg_ref[...] + lb_ref[...]
        o_ref[r * sub:(r + 1) * sub, :] = _silu(y).astype(o_ref.dtype)


def _conv_branch(proj, dw_w, dw_b, ln_g, ln_b, ts=256, sub=64):
    B, S, _ = proj.shape
    H = CONV_HALO
    nh = ts // H
    last = S // H - 1

    def main(col):
        return pl.BlockSpec((None, ts, CONV_W), lambda b, i: (b, i, col))

    def prev(col):
        return pl.BlockSpec((None, H, CONV_W), lambda b, i: (b, jnp.maximum(i * nh - 1, 0), col))

    def nxt(col):
        return pl.BlockSpec((None, H, CONV_W), lambda b, i: (b, jnp.minimum((i + 1) * nh, last), col))

    def const(shape):
        return pl.BlockSpec(shape, lambda b, i: (0, 0))

    return pl.pallas_call(
        functools.partial(_conv_kernel, ts=ts, sub=sub),
        grid=(B, S // ts),
        in_specs=[prev(COL_CA), main(COL_CA), nxt(COL_CA), prev(COL_CB), main(COL_CB), nxt(COL_CB),
                  const((CONV_K, CONV_W)), const((1, CONV_W)), const((1, CONV_W)), const((1, CONV_W))],
        out_specs=pl.BlockSpec((None, ts, CONV_W), lambda b, i: (b, i, 0)),
        out_shape=jax.ShapeDtypeStruct((B, S, CONV_W), BF16),
        scratch_shapes=[pltpu.VMEM((ts + 2 * H, CONV_W), F32)],
        compiler_params=_cparams(("parallel", "parallel")),
        name="conv_branch",
    )(proj, proj, proj, proj, proj, proj, dw_w, dw_b, ln_g, ln_b)


def _rope(x, c, s_up, s_dn):
    n = x.shape[-1]
    half = ROT_DIM // 2
    return x * c + pltpu.roll(x, n - half, 1) * s_up + pltpu.roll(x, half, 1) * s_dn


def _attn_kernel(q_ref, k_ref, v_ref, c_ref, su_ref, sd_ref, lam_ref, g_ref, o_ref, kb_ref, vb_ref,
                 *, tq, lam_init):
    S = k_ref.shape[0]
    kb_ref[...] = _rope(k_ref[...], c_ref[...], su_ref[...], sd_ref[...]).astype(BF16)
    vb_ref[...] = v_ref[...].astype(BF16)

    lp = lam_ref[...]
    lam = (jnp.exp(jnp.sum(lp[0:1] * lp[1:2], axis=-1, keepdims=True))
           - jnp.exp(jnp.sum(lp[2:3] * lp[3:4], axis=-1, keepdims=True)) + lam_init)
    lane = lax.broadcasted_iota(jnp.int32, (tq, 2 * DA_HD), 1)
    first = lane < DA_HD
    scale = DA_HD ** -0.5

    def body(t, carry):
        rows = pl.ds(pl.multiple_of(t * tq, tq), tq)
        q = _rope(q_ref[rows, :], c_ref[rows, :], su_ref[rows, :], sd_ref[rows, :]) * scale
        qq = jnp.concatenate([jnp.where(first, q, 0.0), jnp.where(first, 0.0, q)], axis=0).astype(BF16)
        s = _dot_nt(qq, kb_ref[...])
        p = jnp.exp(s - jnp.max(s, axis=-1, keepdims=True))
        inv = 1.0 / jnp.sum(p, axis=-1, keepdims=True)
        w = p[:tq] * inv[:tq] - (lam * inv[tq:]) * p[tq:]
        o = _dot(w.astype(BF16), vb_ref[...])
        o_ref[rows, :] = (_rms(o, g_ref[...]) * (1.0 - lam_init)).astype(o_ref.dtype)
        return carry

    lax.fori_loop(0, S // tq, body, 0)


def _rope_tables(S):
    half = ROT_DIM // 2
    inv = 1.0 / (ROPE_THETA ** (jnp.arange(0, ROT_DIM, 2, dtype=F32) / ROT_DIM))
    ang = jnp.arange(S, dtype=F32)[:, None] * inv[None, :]
    cos, sin = jnp.cos(ang), jnp.sin(ang)
    pad = jnp.zeros((S, DA_HD - ROT_DIM), F32)
    zero = jnp.zeros((S, half), F32)
    c = jnp.concatenate([cos, cos, pad + 1.0], axis=-1)
    s_up = jnp.concatenate([-sin, zero, pad], axis=-1)
    s_dn = jnp.concatenate([zero, sin, pad], axis=-1)
    return tuple(jnp.tile(t, (1, 2)) for t in (c, s_up, s_dn))


def _diff_attention(proj, rope, lam_params, subln_g, layer, tq=128):
    B, S, _ = proj.shape
    W = 2 * DA_HD
    lam_init = 0.8 - 0.6 * math.exp(-0.3 * layer)

    def head(col):
        return pl.BlockSpec((None, S, W), lambda b, h: (b, 0, col * (CONV_W // W) + h))

    def const(shape):
        return pl.BlockSpec(shape, lambda b, h: (0, 0))

    return pl.pallas_call(
        functools.partial(_attn_kernel, tq=tq, lam_init=lam_init),
        grid=(B, DA_HEADS),
        in_specs=[head(COL_AQ), head(COL_AK), head(COL_AV), const((S, W)), const((S, W)), const((S, W)),
                  const((4, DA_HD)), const((1, W))],
        out_specs=pl.BlockSpec((None, S, W), lambda b, h: (b, 0, h)),
        out_shape=jax.ShapeDtypeStruct((B, S, DA_HEADS * W), BF16),
        scratch_shapes=[pltpu.VMEM((S, W), BF16), pltpu.VMEM((S, W), BF16)],
        compiler_params=_cparams(("parallel", "parallel")),
        name="diff_attention",
    )(proj, proj, proj, *rope, lam_params, subln_g)


def _split3(x):
    hi = x.astype(BF16)
    r1 = x - hi.astype(F32)
    mid = r1.astype(BF16)
    lo = (r1 - mid.astype(F32)).astype(BF16)
    return hi, mid, lo


def _hgrn_direction(q_raw, f_raw, v_all, lb, st_ref, rev):
    C = HG_CHUNK
    NB = C // HG_SUB
    forget = lb + (1.0 - lb) * _sigmoid(f_raw)
    g_all = jnp.log(jnp.maximum(forget, GATE_FLOOR))
    k_all = 1.0 - forget
    q_all = _silu(q_raw)

    r_i = lax.broadcasted_iota(jnp.int32, (C, C), 0)
    c_i = lax.broadcasted_iota(jnp.int32, (C, C), 1)
    tri = jnp.where((c_i >= r_i) if rev else (c_i <= r_i), 1.0, 0.0).astype(BF16)
    hi, mid, lo = _split3(g_all)
    G_all = _dot(tri, hi) + _dot(tri, mid) + _dot(tri, lo)

    blk_t = lax.broadcasted_iota(jnp.int32, (NB, C, HG_DK), 0)
    row_s = lax.broadcasted_iota(jnp.int32, (NB, C, HG_DK), 1)
    if rev:
        valid_off = row_s >= (blk_t + 1) * HG_SUB
    else:
        valid_off = row_s < blk_t * HG_SUB
    t_loc = lax.broadcasted_iota(jnp.int32, (NB, HG_SUB, HG_DK), 1)

    outs = []
    for h in range(HG_HEADS):
        sl = slice(h * HG_DK, (h + 1) * HG_DK)
        G, q, k, v = G_all[:, sl], q_all[:, sl], k_all[:, sl], v_all[:, sl]
        G_end = G[0:1] if rev else G[C - 1:C]
        st = st_ref[h]

        inter = _dot_nt((q * jnp.exp(G)).astype(BF16), st.astype(BF16))
        k_dec = k * jnp.exp(G_end - G)
        st_ref[h] = st * jnp.exp(G_end) + _dot_tn(v.astype(BF16), k_dec.astype(BF16))

        G3 = G.reshape(NB, HG_SUB, HG_DK)
        q3 = q.reshape(NB, HG_SUB, HG_DK)
        k3 = k.reshape(NB, HG_SUB, HG_DK)
        v3 = v.reshape(NB, HG_SUB, HG_DK)

        zero_row = jnp.zeros((1, 1, HG_DK), F32)
        if rev:
            R = jnp.concatenate([G3[1:, 0:1, :], zero_row], axis=0)
        else:
            R = jnp.concatenate([zero_row, G3[:-1, HG_SUB - 1:HG_SUB, :]], axis=0)
        qt = q3 * jnp.exp(jnp.minimum(G3 - R, 0.0))
        kt = jnp.where(valid_off, k[None] * jnp.exp(jnp.minimum(R - G[None], 0.0)), 0.0)
        a_off = jnp.einsum("itk,isk->its", qt.astype(BF16), kt.astype(BF16),
                           preferred_element_type=F32).reshape(C, C)
        intra = _dot(a_off.astype(BF16), v.astype(BF16))

        diag = jnp.zeros((NB, HG_SUB, HG_DK), F32)
        for s in range(HG_SUB):
            Gs = G3[:, s:s + 1, :]
            ks = k3[:, s:s + 1, :]
            vs = v3[:, s:s + 1, :]
            seen = (t_loc <= s) if rev else (t_loc >= s)
            e = jnp.exp(jnp.where(seen, G3 - Gs, -jnp.inf))
            a = jnp.sum(q3 * ks * e, axis=-1, keepdims=True)
            diag = diag + a * vs
        outs.append(inter + intra + diag.reshape(C, HG_DK))
    return jnp.concatenate(outs, axis=-1)


def _hgrn_kernel(qf_ref, ff_ref, vf_ref, qb_ref, fb_ref, vb_ref, lb_ref, of_ref, ob_ref, st_ref):
    @pl.when(pl.program_id(1) == 0)
    def _():
        st_ref[...] = jnp.zeros_like(st_ref)

    of_ref[...] = _hgrn_direction(qf_ref[...], ff_ref[...], vf_ref[...], lb_ref[0:1, :], st_ref.at[0], False)
    ob_ref[...] = _hgrn_direction(qb_ref[...], fb_ref[...], vb_ref[...], lb_ref[1:2, :], st_ref.at[1], True)


def _hgrn(proj, lb):
    B, S, _ = proj.shape
    C = HG_CHUNK
    n = S // C
    W = HG_HEADS * HG_DK

    def fwd(col):
        return pl.BlockSpec((None, C, W), lambda b, c: (b, c, col))

    def bwd(col):
        return pl.BlockSpec((None, C, W), lambda b, c: (b, n - 1 - c, col))

    return pl.pallas_call(
        _hgrn_kernel,
        grid=(B, n),
        in_specs=[fwd(COL_HQ), fwd(COL_HFF), fwd(COL_HI), bwd(COL_HQ), bwd(COL_HFB), bwd(COL_HI),
                  pl.BlockSpec((2, W), lambda b, c: (0, 0))],
        out_specs=[pl.BlockSpec((None, C, W), lambda b, c: (b, c, 0)),
                   pl.BlockSpec((None, C, W), lambda b, c: (b, n - 1 - c, 0))],
        out_shape=[jax.ShapeDtypeStruct((B, S, W), F32), jax.ShapeDtypeStruct((B, S, W), F32)],
        scratch_shapes=[pltpu.VMEM((2, HG_HEADS, HG_DK, HG_DK), F32)],
        compiler_params=_cparams(("parallel", "arbitrary")),
        name="hgrn2",
    )(proj, proj, proj, proj, proj, proj, lb)


def _memkv_kernel(m_ref, g_ref, w_ref, o_ref):
    o_ref[...] = _dot(_rms(m_ref[...], g_ref[...]).astype(BF16), w_ref[...]).astype(o_ref.dtype)


def _mem_kv(mem, g, w):
    B, M, D = mem.shape
    N = w.shape[1]
    return pl.pallas_call(
        _memkv_kernel,
        grid=(B,),
        in_specs=[pl.BlockSpec((None, M, D), lambda b: (b, 0, 0)),
                  pl.BlockSpec((1, D), lambda b: (0, 0)),
                  pl.BlockSpec((D, N), lambda b: (0, 0))],
        out_specs=pl.BlockSpec((None, M, N), lambda b: (b, 0, 0)),
        out_shape=jax.ShapeDtypeStruct((B, M, N), BF16),
        compiler_params=_cparams(("parallel",)),
        name="mem_kv",
    )(mem, g, w)


def _merge_kernel(x_ref, c_ref, a_ref, of_ref, ob_ref, hg_ref, g0_ref, g1_ref, g2_ref, kv_ref,
                  wc_ref, wa_ref, wh_ref, wo_ref, wq_ref, wco_ref, hgn_ref, ncg_ref, o_ref):
    o_sum = of_ref[...] + ob_ref[...]
    gate = _silu(hg_ref[...])
    heads = []
    for h in range(HG_HEADS):
        sl = slice(h * HG_DK, (h + 1) * HG_DK)
        heads.append(_rms(o_sum[:, sl], hgn_ref[...]) * gate[:, sl])
    oh = jnp.concatenate(heads, axis=-1).astype(BF16)

    y_c = _dot(c_ref[...], wc_ref[...])
    y_a = _dot(a_ref[...], wa_ref[...])
    y_h = _dot(oh, wh_ref[...])
    merged = _sigmoid(g0_ref[...]) * y_c + _sigmoid(g1_ref[...]) * y_a + _sigmoid(g2_ref[...]) * y_h
    x1 = x_ref[...] + _dot(merged.astype(BF16), wo_ref[...])

    hn = _rms(x1, ncg_ref[...]).astype(BF16)
    q = (_dot(hn, wq_ref[...]) * (X_HD ** -0.5)).astype(BF16)
    outs = []
    for h in range(X_HEADS):
        k = kv_ref[:, h * X_HD:(h + 1) * X_HD]
        v = kv_ref[:, D_MODEL + h * X_HD:D_MODEL + (h + 1) * X_HD]
        s = _dot_nt(q[:, h * X_HD:(h + 1) * X_HD], k)
        p = jnp.exp(s - jnp.max(s, axis=-1, keepdims=True))
        p = p * (1.0 / jnp.sum(p, axis=-1, keepdims=True))
        outs.append(_dot(p.astype(BF16), v).astype(BF16))
    o = jnp.concatenate(outs, axis=-1)
    o_ref[...] = x1 + _dot(o, wco_ref[...])


def _merge_cross(x, proj, c, a, o_f, o_b, kv, wc, wa, wh, wo, wq, wco, hgn, ncg, tm=256):
    B, S, D = x.shape
    M = kv.shape[1]
    W = CONV_W

    def tok(width, col=0):
        return pl.BlockSpec((None, tm, width), lambda b, i: (b, i, col))

    def const(shape):
        return pl.BlockSpec(shape, lambda b, i: (0, 0))

    return pl.pallas_call(
        _merge_kernel,
        grid=(B, S // tm),
        in_specs=[tok(D), tok(W), tok(W), tok(W), tok(W), tok(W, COL_HG),
                  tok(D, COL_GATE_1024), tok(D, COL_GATE_1024 + 1), tok(D, COL_GATE_1024 + 2),
                  pl.BlockSpec((None, M, 2 * D), lambda b, i: (b, 0, 0)),
                  const((W, D)), const((W, D)), const((W, D)), const((D, D)), const((D, D)), const((D, D)),
                  const((1, HG_DK)), const((1, D))],
        out_specs=tok(D),
        out_shape=jax.ShapeDtypeStruct((B, S, D), F32),
        compiler_params=_cparams(("parallel", "parallel")),
        name="merge_cross",
    )(x, c, a, o_f, o_b, proj, proj, proj, proj, kv, wc, wa, wh, wo, wq, wco, hgn, ncg)


def _ffn_kernel(xp_ref, x_ref, xn_ref, g_ref, wa_ref, wb_ref, dwa_ref, dwb_ref, ba_ref, bb_ref, wd_ref,
                fg_ref, o_ref, hn_ref, acc_ref, *, tm, final):
    i = pl.program_id(1)
    n = pl.num_programs(1)
    f = pl.program_id(2)
    nf = pl.num_programs(2)
    H = FFN_HALO

    @pl.when(f == 0)
    def _():
        g = g_ref[...]
        hn_ref[0:H, :] = jnp.where(i > 0, _rms(xp_ref[...], g), 0.0).astype(BF16)
        hn_ref[H:H + tm, :] = _rms(x_ref[...], g).astype(BF16)
        hn_ref[H + tm:H + tm + H, :] = jnp.where(i < n - 1, _rms(xn_ref[...], g), 0.0).astype(BF16)
        acc_ref[...] = jnp.zeros_like(acc_ref)

    hn = hn_ref[...]
    rows = tm + 2 * H

    def conv(u, dw_ref, b_ref):
        up = pltpu.roll(u, 1, 0)[H:H + tm]
        dn = pltpu.roll(u, rows - 1, 0)[H:H + tm]
        return dw_ref[0:1, :] * up + dw_ref[1:2, :] * u[H:H + tm] + dw_ref[2:3, :] * dn + b_ref[...]

    a = conv(_dot(hn, wa_ref[...]), dwa_ref, ba_ref)
    b = conv(_dot(hn, wb_ref[...]), dwb_ref, bb_ref)
    acc_ref[...] += _dot((_silu(a) * b).astype(BF16), wd_ref[...])

    @pl.when(f == nf - 1)
    def _():
        y = x_ref[...] + acc_ref[...]
        if final:
            y = _rms(y, fg_ref[...])
        o_ref[...] = y


def _conv_ffn(x, g, w_up, dw_w, dw_b, w_down, final_g, final, tm=512):
    B, S, D = x.shape
    H = FFN_HALO
    nh = tm // H
    last = S // H - 1
    fc = FF_CHUNK
    nf = D_FF // fc

    def const(shape):
        return pl.BlockSpec(shape, lambda b, i, f: (0, 0))

    return pl.pallas_call(
        functools.partial(_ffn_kernel, tm=tm, final=final),
        grid=(B, S // tm, nf),
        in_specs=[
            pl.BlockSpec((None, H, D), lambda b, i, f: (b, jnp.maximum(i * nh - 1, 0), 0)),
            pl.BlockSpec((None, tm, D), lambda b, i, f: (b, i, 0)),
            pl.BlockSpec((None, H, D), lambda b, i, f: (b, jnp.minimum((i + 1) * nh, last), 0)),
            const((1, D)),
            pl.BlockSpec((D, fc), lambda b, i, f: (0, f)),
            pl.BlockSpec((D, fc), lambda b, i, f: (0, nf + f)),
            pl.BlockSpec((3, fc), lambda b, i, f: (0, f)),
            pl.BlockSpec((3, fc), lambda b, i, f: (0, nf + f)),
            pl.BlockSpec((1, fc), lambda b, i, f: (0, f)),
            pl.BlockSpec((1, fc), lambda b, i, f: (0, nf + f)),
            pl.BlockSpec((fc, D), lambda b, i, f: (f, 0)),
            const((1, D)),
        ],
        out_specs=pl.BlockSpec((None, tm, D), lambda b, i, f: (b, i, 0)),
        out_shape=jax.ShapeDtypeStruct((B, S, D), F32),
        scratch_shapes=[pltpu.VMEM((tm + 2 * H, D), BF16), pltpu.VMEM((tm, D), F32)],
        compiler_params=_cparams(("parallel", "parallel", "arbitrary")),
        name="conv_ffn",
    )(x, x, x, g, w_up, w_up, dw_w, dw_w, dw_b, dw_b, w_down, final_g)


def _trunk(x, mem, p, lbs):
    S = x.shape[1]
    rope = _rope_tables(S)
    for l in range(DEPTH):
        proj = _inproj(x, p["norm_mix_g"][l], p["w_in"][l])
        c = _conv_branch(proj, p["conv_dw_w"][l], p["conv_dw_b"][l], p["conv_ln_g"][l], p["conv_ln_b"][l])
        a = _diff_attention(proj, rope, p["attn_lambda"][l], p["attn_subln_g"][l], l)
        o_f, o_b = _hgrn(proj, lbs[l])
        kv = _mem_kv(mem, p["norm_mem_g"][l], p["w_ckv"][l])
        x = _merge_cross(x, proj, c, a, o_f, o_b, kv, p["w_conv_out"][l], p["w_attn_out"][l],
                         p["w_hg_out"][l], p["w_o"][l], p["w_cq"][l], p["w_co"][l],
                         p["hg_norm_g"][l], p["norm_cross_g"][l])
        x = _conv_ffn(x, p["norm_ffn_g"][l], p["w_up"][l], p["ffn_dw_w"][l], p["ffn_dw_b"][l],
                      p["w_down"][l], p["final_norm_g"], final=(l == DEPTH - 1))
    return x


def _prepare(norm_mix_g, w_in, conv_dw_w, conv_dw_b, conv_ln_g, conv_ln_b, w_conv_out, attn_lambda,
             attn_subln_g, w_attn_out, hg_lb_param, hg_norm_g, w_hg_out, w_o, norm_cross_g, norm_mem_g,
             w_cq, w_ckv, w_co, norm_ffn_g, w_up, ffn_dw_w, ffn_dw_b, w_down, final_norm_g):
    row = lambda t: t.astype(F32)[:, None, :]
    p = {
        "norm_mix_g": row(norm_mix_g), "w_in": w_in.astype(BF16),
        "conv_dw_w": conv_dw_w.astype(F32), "conv_dw_b": row(conv_dw_b),
        "conv_ln_g": row(conv_ln_g), "conv_ln_b": row(conv_ln_b), "w_conv_out": w_conv_out.astype(BF16),
        "attn_lambda": attn_lambda.astype(F32), "attn_subln_g": row(attn_subln_g),
        "w_attn_out": w_attn_out.astype(BF16),
        "hg_norm_g": row(hg_norm_g), "w_hg_out": w_hg_out.astype(BF16), "w_o": w_o.astype(BF16),
        "norm_cross_g": row(norm_cross_g), "norm_mem_g": row(norm_mem_g),
        "w_cq": w_cq.astype(BF16), "w_ckv": w_ckv.astype(BF16), "w_co": w_co.astype(BF16),
        "norm_ffn_g": row(norm_ffn_g), "w_up": w_up.astype(BF16),
        "ffn_dw_w": ffn_dw_w.astype(F32), "ffn_dw_b": row(ffn_dw_b), "w_down": w_down.astype(BF16),
        "final_norm_g": final_norm_g.astype(F32)[None, :],
    }
    lp = jax.nn.softmax(hg_lb_param.astype(F32), axis=0)
    lbs = jnp.cumsum(lp, axis=0) - lp[0:1]
    return p, lbs


def kernel(x_prompt, x_sample, mem_prompt, mem_sample, norm_mix_g, w_in, conv_dw_w, conv_dw_b, conv_ln_g, conv_ln_b, w_conv_out, attn_lambda, attn_subln_g, w_attn_out, hg_lb_param, hg_norm_g, w_hg_out, w_o, norm_cross_g, norm_mem_g, w_cq, w_ckv, w_co, norm_ffn_g, w_up, ffn_dw_w, ffn_dw_b, w_down, final_norm_g):
    p, lbs = _prepare(norm_mix_g, w_in, conv_dw_w, conv_dw_b, conv_ln_g, conv_ln_b, w_conv_out, attn_lambda,
                      attn_subln_g, w_attn_out, hg_lb_param, hg_norm_g, w_hg_out, w_o, norm_cross_g,
                      norm_mem_g, w_cq, w_ckv, w_co, norm_ffn_g, w_up, ffn_dw_w, ffn_dw_b, w_down,
                      final_norm_g)
    y_prompt = _trunk(x_prompt, mem_prompt, p, lbs)
    y_sample = _trunk(x_sample, mem_sample, p, lbs)
    return (y_prompt, y_sample)
```

```python
import functools
import math

import jax
import jax.numpy as jnp
from jax import lax
from jax.experimental import pallas as pl
from jax.experimental.pallas import tpu as pltpu

F32 = jnp.float32
BF16 = jnp.bfloat16

D_MODEL = 1024
DEPTH = 4
CONV_W = 512
CONV_K = 31
CONV_HALO = 16
DA_HEADS = 4
DA_HD = 64
ROT_DIM = 16
ROPE_THETA = 500000.0
ATTN_KEY_CHUNK = 512
HG_HEADS = 4
HG_DK = 128
HG_CHUNK = 64
HG_STEP_CHUNKS = 2
HG_SUB = 8
GATE_FLOOR = 1e-30
X_HEADS = 4
X_HD = 256
D_FF = 2816
FF_CHUNK = 1408
FFN_HALO = 8
N_IN = 8192
EPS = 1e-6

GATE_CHUNK_W = 1024
GATE_CHUNK = 3
COL_CA, COL_CB, COL_AQ, COL_AK, COL_AV, COL_HQ, COL_HI, COL_HG = range(8)
COL_GATE_1024 = 4
LOG2E = 1.4426950408889634

VMEM_LIMIT = 56 * 1024 * 1024


def _cparams(sem):
    return pltpu.CompilerParams(dimension_semantics=sem, vmem_limit_bytes=VMEM_LIMIT)


def _rms(x, g):
    ms = jnp.mean(x * x, axis=-1, keepdims=True)
    return x * lax.rsqrt(ms + EPS) * g


def _sigmoid(x):
    return 1.0 / (1.0 + jnp.exp(-x))


def _silu(x):
    return x * _sigmoid(x)


def _dot(a, b):
    return jnp.dot(a, b, preferred_element_type=F32)


def _dot_nt(a, b):
    return lax.dot_general(a, b, (((1,), (1,)), ((), ())), preferred_element_type=F32)


def _dot_tn(a, b):
    return lax.dot_general(a, b, (((0,), (0,)), ((), ())), preferred_element_type=F32)


def _inproj_kernel(x_ref, g_ref, w_ref, o_ref, gate_ref):
    xn = _rms(x_ref[...], g_ref[...]).astype(BF16)
    tn = GATE_CHUNK_W
    for j in range(N_IN // tn):
        y = _dot(xn, w_ref[:, j * tn:(j + 1) * tn])
        if j == GATE_CHUNK:
            gate_ref[...] = y
        else:
            jo = j - (j > GATE_CHUNK)
            o_ref[:, jo * tn:(jo + 1) * tn] = y.astype(o_ref.dtype)


def _inproj(x, g, w, tm=512):
    B, S, D = x.shape
    N = w.shape[1]
    return pl.pallas_call(
        _inproj_kernel,
        grid=(B, S // tm),
        in_specs=[
            pl.BlockSpec((None, tm, D), lambda b, i: (b, i, 0)),
            pl.BlockSpec((1, D), lambda b, i: (0, 0)),
            pl.BlockSpec((D, N), lambda b, i: (0, 0), pipeline_mode=pl.Buffered(1)),
        ],
        out_specs=[pl.BlockSpec((None, tm, N - GATE_CHUNK_W), lambda b, i: (b, i, 0)),
                   pl.BlockSpec((None, tm, GATE_CHUNK_W), lambda b, i: (b, i, 0))],
        out_shape=[jax.ShapeDtypeStruct((B, S, N - GATE_CHUNK_W), BF16),
                   jax.ShapeDtypeStruct((B, S, GATE_CHUNK_W), F32)],
        compiler_params=_cparams(("parallel", "parallel")),
        name="inproj",
    )(x, g, w)


def _conv_kernel(ap_ref, a_ref, an_ref, bp_ref, b_ref, bn_ref, w_ref, bias_ref, lg_ref, lb_ref,
                 o_ref, ext_ref, sh_ref, *, ts, sub):
    i = pl.program_id(1)
    n = pl.num_programs(1)
    H = CONV_HALO
    SUBL = 8

    def glu(a, b):
        return a.astype(F32) * _sigmoid(b.astype(F32))

    ext_ref[0:H, :] = jnp.where(i > 0, glu(ap_ref[...], bp_ref[...]), 0.0)
    ext_ref[H:H + ts, :] = glu(a_ref[...], b_ref[...])
    ext_ref[H + ts:H + ts + H, :] = jnp.where(i < n - 1, glu(an_ref[...], bn_ref[...]), 0.0)

    rows_sh = sh_ref.shape[1]
    for r in range(SUBL):
        sh_ref[r] = ext_ref[pl.ds(r, rows_sh), :]

    off = H - CONV_K // 2
    for r in range(ts // sub):
        acc = jnp.zeros((sub, CONV_W), F32)
        for j in range(CONV_K):
            start = j + off
            acc = acc + w_ref[j:j + 1, :] * sh_ref[start % SUBL, pl.ds(r * sub + start - start % SUBL, sub), :]
        acc = acc + bias_ref[...]
        mu = jnp.mean(acc, axis=-1, keepdims=True)
        d = acc - mu
        var = jnp.mean(d * d, axis=-1, keepdims=True)
        y = d * lax.rsqrt(var + EPS) * lg_ref[...] + lb_ref[...]
        o_ref[r * sub:(r + 1) * sub, :] = _silu(y).astype(o_ref.dtype)


def _conv_branch(proj, dw_w, dw_b, ln_g, ln_b, ts=512, sub=64):
    B, S, _ = proj.shape
    H = CONV_HALO
    nh = ts // H
    last = S // H - 1

    def main(col):
        return pl.BlockSpec((None, ts, CONV_W), lambda b, i: (b, i, col))

    def prev(col):
        return pl.BlockSpec((None, H, CONV_W), lambda b, i: (b, jnp.maximum(i * nh - 1, 0), col))

    def nxt(col):
        return pl.BlockSpec((None, H, CONV_W), lambda b, i: (b, jnp.minimum((i + 1) * nh, last), col))

    def const(shape):
        return pl.BlockSpec(shape, lambda b, i: (0, 0))

    return pl.pallas_call(
        functools.partial(_conv_kernel, ts=ts, sub=sub),
        grid=(B, S // ts),
        in_specs=[prev(COL_CA), main(COL_CA), nxt(COL_CA), prev(COL_CB), main(COL_CB), nxt(COL_CB),
                  const((CONV_K, CONV_W)), const((1, CONV_W)), const((1, CONV_W)), const((1, CONV_W))],
        out_specs=pl.BlockSpec((None, ts, CONV_W), lambda b, i: (b, i, 0)),
        out_shape=jax.ShapeDtypeStruct((B, S, CONV_W), BF16),
        scratch_shapes=[pltpu.VMEM((ts + 2 * H, CONV_W), F32),
                        pltpu.VMEM((8, ts + 2 * H - 8, CONV_W), F32)],
        compiler_params=_cparams(("parallel", "parallel")),
        name="conv_branch",
    )(proj, proj, proj, proj, proj, proj, dw_w, dw_b, ln_g, ln_b)


def _rope(x, c, s_up, s_dn):
    n = x.shape[-1]
    half = ROT_DIM // 2
    return x * c + pltpu.roll(x, n - half, 1) * s_up + pltpu.roll(x, half, 1) * s_dn


def _attn_kernel(q_ref, k_ref, v_ref, c_ref, su_ref, sd_ref, lam_ref, g_ref, o_ref, kb_ref, vx_ref, qq_ref,
                 sa_ref, sb_ref, *, tq, lam_init):
    S = k_ref.shape[0]
    W = 2 * DA_HD
    nt = S // tq
    kb_ref[...] = _rope(k_ref[...].astype(F32), c_ref[...], su_ref[...], sd_ref[...]).astype(BF16)
    vx_ref[:, :W] = v_ref[...]
    vx_ref[:, W:] = jnp.where(lax.broadcasted_iota(jnp.int32, (S, W), 1) == 0, 1.0, 0.0).astype(BF16)

    lp = lam_ref[...]
    lam = (jnp.exp(jnp.sum(lp[0:1] * lp[1:2], axis=-1, keepdims=True))
           - jnp.exp(jnp.sum(lp[2:3] * lp[3:4], axis=-1, keepdims=True)) + lam_init)
    scale = DA_HD ** -0.5 * LOG2E

    def rows_of(t):
        return pl.ds(pl.multiple_of(t * tq, tq), tq)

    def score_chunk(qq, c, s_ref, mx):
        keys = slice(c * ATTN_KEY_CHUNK, (c + 1) * ATTN_KEY_CHUNK)
        s = _dot_nt(qq, kb_ref[keys, :])
        s_ref[:, keys] = s
        for i in range(ATTN_KEY_CHUNK // W):
            mx = jnp.maximum(mx, s[:, i * W:(i + 1) * W])
        return mx

    def write_out(t, r):
        inv = 1.0 / r[:, W:W + 1]
        o = r[:tq, :W] * inv[:tq] - (lam * inv[tq:]) * r[tq:, :W]
        o_ref[rows_of(t), :] = (_rms(o, g_ref[...]) * (1.0 - lam_init)).astype(o_ref.dtype)

    def step(t_next, nxt_ref, cur_ref, m_cur, t_prev, r_prev):
        qq = qq_ref[t_next]
        mx = jnp.full((2 * tq, W), -jnp.inf, F32)
        r = jnp.zeros((2 * tq, 2 * W), F32)
        for c in range(S // ATTN_KEY_CHUNK):
            keys = slice(c * ATTN_KEY_CHUNK, (c + 1) * ATTN_KEY_CHUNK)
            mx = score_chunk(qq, c, nxt_ref, mx)
            p = jnp.exp2(cur_ref[:, keys] - m_cur).astype(BF16)
            r = r + _dot(p, vx_ref[keys, :])
            if c == 0:
                write_out(t_prev, r_prev)
        return jnp.max(mx, axis=-1, keepdims=True), r

    q_all = _rope(q_ref[...].astype(F32), c_ref[...], su_ref[...], sd_ref[...]) * scale
    lane_all = lax.broadcasted_iota(jnp.int32, (S, W), 1) < DA_HD
    q_one = jnp.where(lane_all, q_all, 0.0).astype(BF16)
    q_two = jnp.where(lane_all, 0.0, q_all).astype(BF16)
    for t in range(nt):
        qq_ref[t, :tq, :] = q_one[t * tq:(t + 1) * tq]
        qq_ref[t, tq:, :] = q_two[t * tq:(t + 1) * tq]

    qq0 = qq_ref[0]
    mx0 = jnp.full((2 * tq, W), -jnp.inf, F32)
    for c in range(S // ATTN_KEY_CHUNK):
        mx0 = score_chunk(qq0, c, sa_ref, mx0)
    m0 = jnp.max(mx0, axis=-1, keepdims=True)

    def body(j, carry):
        m_a, r_prev = carry
        t0 = 2 * j
        m_b, r_a = step(t0 + 1, sb_ref, sa_ref, m_a, jnp.maximum(t0 - 1, 0), r_prev)
        m_a, r_b = step(jnp.minimum(t0 + 2, nt - 1), sa_ref, sb_ref, m_b, t0, r_a)
        return m_a, r_b

    _, r_last = lax.fori_loop(0, nt // 2, body, (m0, jnp.ones((2 * tq, 2 * W), F32)))
    write_out(nt - 1, r_last)


def _rope_tables(S):
    half = ROT_DIM // 2
    inv = 1.0 / (ROPE_THETA ** (jnp.arange(0, ROT_DIM, 2, dtype=F32) / ROT_DIM))
    ang = jnp.arange(S, dtype=F32)[:, None] * inv[None, :]
    cos, sin = jnp.cos(ang), jnp.sin(ang)
    pad = jnp.zeros((S, DA_HD - ROT_DIM), F32)
    zero = jnp.zeros((S, half), F32)
    c = jnp.concatenate([cos, cos, pad + 1.0], axis=-1)
    s_up = jnp.concatenate([-sin, zero, pad], axis=-1)
    s_dn = jnp.concatenate([zero, sin, pad], axis=-1)
    return tuple(jnp.tile(t, (1, 2)) for t in (c, s_up, s_dn))


def _diff_attention(proj, rope, lam_params, subln_g, layer, tq=256):
    B, S, _ = proj.shape
    W = 2 * DA_HD
    lam_init = 0.8 - 0.6 * math.exp(-0.3 * layer)

    def head(col):
        return pl.BlockSpec((None, S, W), lambda b, h: (b, 0, col * (CONV_W // W) + h))

    def const(shape):
        return pl.BlockSpec(shape, lambda b, h: (0, 0))

    return pl.pallas_call(
        functools.partial(_attn_kernel, tq=tq, lam_init=lam_init),
        grid=(B, DA_HEADS),
        in_specs=[head(COL_AQ), head(COL_AK), head(COL_AV), const((S, W)), const((S, W)), const((S, W)),
                  const((4, DA_HD)), const((1, W))],
        out_specs=pl.BlockSpec((None, S, W), lambda b, h: (b, 0, h)),
        out_shape=jax.ShapeDtypeStruct((B, S, DA_HEADS * W), BF16),
        scratch_shapes=[pltpu.VMEM((S, W), BF16), pltpu.VMEM((S, 2 * W), BF16),
                        pltpu.VMEM((S // tq, 2 * tq, W), BF16),
                        pltpu.VMEM((2 * tq, S), F32), pltpu.VMEM((2 * tq, S), F32)],
        compiler_params=_cparams(("parallel", "parallel")),
        name="diff_attention",
    )(proj, proj, proj, *rope, lam_params, subln_g)


def _split3(x):
    hi = x.astype(BF16)
    r1 = x - hi.astype(F32)
    mid = r1.astype(BF16)
    lo = (r1 - mid.astype(F32)).astype(BF16)
    return hi, mid, lo


def _boundary_rows(G, b, rev):
    C, K = G.shape
    e = b if rev else b - 1
    if 2 * b >= HG_SUB:
        G3 = G.reshape(C // (2 * b), 2 * b, K)
        return jnp.broadcast_to(G3[:, e:e + 1, :], G3.shape).reshape(C, K)
    assert b == 2
    G3 = G.reshape(C // HG_SUB, HG_SUB, K)
    lo = jnp.broadcast_to(G3[:, e:e + 1, :], G3.shape)
    hi = jnp.broadcast_to(G3[:, 4 + e:5 + e, :], G3.shape)
    row = lax.broadcasted_iota(jnp.int32, G3.shape, 1)
    return jnp.where(row < 4, lo, hi).reshape(C, K)


def _hgrn_masks(rev):
    C = HG_CHUNK
    r_i = lax.broadcasted_iota(jnp.int32, (C, C), 0)
    c_i = lax.broadcasted_iota(jnp.int32, (C, C), 1)
    row_k = lax.broadcasted_iota(jnp.int32, (C, HG_DK), 0)
    tri = jnp.where((c_i >= r_i) if rev else (c_i <= r_i), 1.0, 0.0).astype(BF16)
    levels = []
    b = C // 2
    while b >= 1:
        t_later = (r_i // b) % 2 == (0 if rev else 1)
        sibling = (c_i // b) == (r_i // b) + (1 if rev else -1)
        q_rows = (row_k // b) % 2 == (0 if rev else 1)
        levels.append((b, t_later & sibling, q_rows))
        b //= 2
    return tri, levels


def _hgrn_prep(q_raw, f_raw, v_raw, lb, tri):
    forget = lb + (1.0 - lb) * _sigmoid(f_raw)
    decay = jnp.maximum(forget, GATE_FLOOR)
    hi, mid, lo = _split3(jnp.log(decay))
    G2 = (_dot(tri, hi) + _dot(tri, mid) + _dot(tri, lo)) * LOG2E
    return dict(G2=G2, q=_silu(q_raw.astype(F32)), k=1.0 - forget, v=v_raw, decay=decay)


def _hgrn_scores(t, levels, rev):
    C = HG_CHUNK
    out = []
    for h in range(HG_HEADS):
        sl = slice(h * HG_DK, (h + 1) * HG_DK)
        G2, q, k, decay = t["G2"][:, sl], t["q"][:, sl], t["k"][:, sl], t["decay"][:, sl]
        a = jnp.zeros((C, C), F32)
        for b, pair_mask, q_rows in levels:
            if b == 1:
                qt, kt = q * decay, k
            else:
                d = G2 - _boundary_rows(G2, b, rev)
                e = jnp.exp2(jnp.where(q_rows, d, -d))
                qt, kt = q * e, k * e
            a = a + jnp.where(pair_mask, _dot_nt(qt.astype(BF16), kt.astype(BF16)), 0.0)
        out.append(a.astype(BF16))
    return out


def _hgrn_finish(t, scores, states, rev):
    C = HG_CHUNK
    outs, new_states = [], []
    for h in range(HG_HEADS):
        sl = slice(h * HG_DK, (h + 1) * HG_DK)
        G2, q, k, v = t["G2"][:, sl], t["q"][:, sl], t["k"][:, sl], t["v"][:, sl]
        G2_end = G2[0:1] if rev else G2[C - 1:C]
        st = states[h]
        inter = _dot_nt((q * jnp.exp2(G2)).astype(BF16), st.astype(BF16))
        k_dec = k * jnp.exp2(G2_end - G2)
        new_states.append(st * jnp.exp2(G2_end) + _dot_tn(v, k_dec.astype(BF16)))
        intra = _dot(scores[h], v)
        diag = jnp.sum(q * k, axis=-1, keepdims=True) * v.astype(F32)
        outs.append(inter + intra + diag)
    return jnp.concatenate(outs, axis=-1), new_states


def _hgrn_kernel(qf_ref, ff_ref, vf_ref, qb_ref, fb_ref, vb_ref, lb_ref, of_ref, ob_ref, stf_ref, stb_ref):
    @pl.when(pl.program_id(1) == 0)
    def _():
        stf_ref[...] = jnp.zeros_like(stf_ref)
        stb_ref[...] = jnp.zeros_like(stb_ref)

    C = HG_CHUNK
    dirs = [dict(rev=False, q=qf_ref, f=ff_ref, v=vf_ref, lb=lb_ref[0:1, :], o=of_ref, st=stf_ref,
                 order=range(HG_STEP_CHUNKS)),
            dict(rev=True, q=qb_ref, f=fb_ref, v=vb_ref, lb=lb_ref[1:2, :], o=ob_ref, st=stb_ref,
                 order=range(HG_STEP_CHUNKS - 1, -1, -1))]
    for d in dirs:
        d["tri"], d["levels"] = _hgrn_masks(d["rev"])
        d["rows"] = [slice(j * C, (j + 1) * C) for j in d["order"]]
        d["chunks"] = [_hgrn_prep(d["q"][r, :], d["f"][r, :], d["v"][r, :], d["lb"], d["tri"])
                       for r in d["rows"]]
    for d in dirs:
        d["scores"] = [_hgrn_scores(t, d["levels"], d["rev"]) for t in d["chunks"]]
    for d in dirs:
        d["states"] = [d["st"][h] for h in range(HG_HEADS)]
    for j in range(HG_STEP_CHUNKS):
        for d in dirs:
            d["o"][d["rows"][j], :], d["states"] = _hgrn_finish(d["chunks"][j], d["scores"][j], d["states"],
                                                                d["rev"])
    for d in dirs:
        for h in range(HG_HEADS):
            d["st"][h] = d["states"][h]


def _hgrn(proj, gates, lb):
    B, S, _ = proj.shape
    C = HG_CHUNK * HG_STEP_CHUNKS
    n = S // C
    W = HG_HEADS * HG_DK

    def fwd(col):
        return pl.BlockSpec((None, C, W), lambda b, c: (b, c, col))

    def bwd(col):
        return pl.BlockSpec((None, C, W), lambda b, c: (b, n - 1 - c, col))

    return pl.pallas_call(
        _hgrn_kernel,
        grid=(B, n),
        in_specs=[fwd(COL_HQ), fwd(0), fwd(COL_HI), bwd(COL_HQ), bwd(1), bwd(COL_HI),
                  pl.BlockSpec((2, W), lambda b, c: (0, 0))],
        out_specs=[pl.BlockSpec((None, C, W), lambda b, c: (b, c, 0)),
                   pl.BlockSpec((None, C, W), lambda b, c: (b, n - 1 - c, 0))],
        out_shape=[jax.ShapeDtypeStruct((B, S, W), F32), jax.ShapeDtypeStruct((B, S, W), F32)],
        scratch_shapes=[pltpu.VMEM((HG_HEADS, HG_DK, HG_DK), F32), pltpu.VMEM((HG_HEADS, HG_DK, HG_DK), F32)],
        compiler_params=_cparams(("parallel", "arbitrary")),
        name="hgrn2",
    )(proj, gates, proj, proj, gates, proj, lb)


def _memkv_kernel(m_ref, g_ref, w_ref, o_ref):
    o_ref[...] = _dot(_rms(m_ref[...], g_ref[...]).astype(BF16), w_ref[...]).astype(o_ref.dtype)


def _mem_kv(mem, g, w):
    B, M, D = mem.shape
    N = w.shape[1]
    return pl.pallas_call(
        _memkv_kernel,
        grid=(B,),
        in_specs=[pl.BlockSpec((None, M, D), lambda b: (b, 0, 0)),
                  pl.BlockSpec((1, D), lambda b: (0, 0)),
                  pl.BlockSpec((D, N), lambda b: (0, 0))],
        out_specs=pl.BlockSpec((None, M, N), lambda b: (b, 0, 0)),
        out_shape=jax.ShapeDtypeStruct((B, M, N), BF16),
        compiler_params=_cparams(("parallel",)),
        name="mem_kv",
    )(mem, g, w)


def _merge_kernel(x_ref, c_ref, a_ref, of_ref, ob_ref, hg_ref, g0_ref, g1_ref, g2_ref, kv_ref,
                  wc_ref, wa_ref, wh_ref, wo_ref, wq_ref, wco_ref, hgn_ref, ncg_ref, o_ref):
    o_sum = of_ref[...] + ob_ref[...]
    gate = _silu(hg_ref[...].astype(F32))
    heads = []
    for h in range(HG_HEADS):
        sl = slice(h * HG_DK, (h + 1) * HG_DK)
        heads.append(_rms(o_sum[:, sl], hgn_ref[...]) * gate[:, sl])
    oh = jnp.concatenate(heads, axis=-1).astype(BF16)

    y_c = _dot(c_ref[...], wc_ref[...])
    y_a = _dot(a_ref[...], wa_ref[...])
    y_h = _dot(oh, wh_ref[...])
    merged = (_sigmoid(g0_ref[...].astype(F32)) * y_c + _sigmoid(g1_ref[...].astype(F32)) * y_a
              + _sigmoid(g2_ref[...].astype(F32)) * y_h)
    x1 = x_ref[...] + _dot(merged.astype(BF16), wo_ref[...])

    hn = _rms(x1, ncg_ref[...]).astype(BF16)
    q = (_dot(hn, wq_ref[...]) * (X_HD ** -0.5)).astype(BF16)
    outs = []
    for h in range(X_HEADS):
        k = kv_ref[:, h * X_HD:(h + 1) * X_HD]
        v = kv_ref[:, D_MODEL + h * X_HD:D_MODEL + (h + 1) * X_HD]
        s = _dot_nt(q[:, h * X_HD:(h + 1) * X_HD], k)
        p = jnp.exp(s - jnp.max(s, axis=-1, keepdims=True))
        p = p * (1.0 / jnp.sum(p, axis=-1, keepdims=True))
        outs.append(_dot(p.astype(BF16), v).astype(BF16))
    o = jnp.concatenate(outs, axis=-1)
    o_ref[...] = x1 + _dot(o, wco_ref[...])


def _merge_cross(x, proj, c, a, o_f, o_b, kv, wc, wa, wh, wo, wq, wco, hgn, ncg, tm=256):
    B, S, D = x.shape
    M = kv.shape[1]
    W = CONV_W

    def tok(width, col=0):
        return pl.BlockSpec((None, tm, width), lambda b, i: (b, i, col))

    def const(shape):
        return pl.BlockSpec(shape, lambda b, i: (0, 0))

    return pl.pallas_call(
        _merge_kernel,
        grid=(B, S // tm),
        in_specs=[tok(D), tok(W), tok(W), tok(W), tok(W), tok(W, COL_HG),
                  tok(D, COL_GATE_1024), tok(D, COL_GATE_1024 + 1), tok(D, COL_GATE_1024 + 2),
                  pl.BlockSpec((None, M, 2 * D), lambda b, i: (b, 0, 0)),
                  const((W, D)), const((W, D)), const((W, D)), const((D, D)), const((D, D)), const((D, D)),
                  const((1, HG_DK)), const((1, D))],
        out_specs=tok(D),
        out_shape=jax.ShapeDtypeStruct((B, S, D), F32),
        compiler_params=_cparams(("parallel", "parallel")),
        name="merge_cross",
    )(x, c, a, o_f, o_b, proj, proj, proj, proj, kv, wc, wa, wh, wo, wq, wco, hgn, ncg)


def _ffn_kernel(xp_ref, x_ref, xn_ref, g_ref, wa_ref, wb_ref, dwa_ref, dwb_ref, ba_ref, bb_ref, wd_ref,
                fg_ref, o_ref, hn_ref, acc_ref, *, tm, final):
    i = pl.program_id(1)
    n = pl.num_programs(1)
    f = pl.program_id(2)
    nf = pl.num_programs(2)
    H = FFN_HALO

    @pl.when(f == 0)
    def _():
        g = g_ref[...]
        hn_ref[0:H, :] = jnp.where(i > 0, _rms(xp_ref[...], g), 0.0).astype(BF16)
        hn_ref[H:H + tm, :] = _rms(x_ref[...], g).astype(BF16)
        hn_ref[H + tm:H + tm + H, :] = jnp.where(i < n - 1, _rms(xn_ref[...], g), 0.0).astype(BF16)
        acc_ref[...] = jnp.zeros_like(acc_ref)

    hn = hn_ref[...]
    rows = tm + 2 * H

    def conv(u, dw_ref, b_ref):
        up = pltpu.roll(u, 1, 0)[H:H + tm]
        dn = pltpu.roll(u, rows - 1, 0)[H:H + tm]
        return dw_ref[0:1, :] * up + dw_ref[1:2, :] * u[H:H + tm] + dw_ref[2:3, :] * dn + b_ref[...]

    a = conv(_dot(hn, wa_ref[...]), dwa_ref, ba_ref)
    b = conv(_dot(hn, wb_ref[...]), dwb_ref, bb_ref)
    acc_ref[...] += _dot((_silu(a) * b).astype(BF16), wd_ref[...])

    @pl.when(f == nf - 1)
    def _():
        y = x_ref[...] + acc_ref[...]
        if final:
            y = _rms(y, fg_ref[...])
        o_ref[...] = y


def _conv_ffn(x, g, w_up, dw_w, dw_b, w_down, final_g, final, tm=512):
    B, S, D = x.shape
    H = FFN_HALO
    nh = tm // H
    last = S // H - 1
    fc = FF_CHUNK
    nf = D_FF // fc

    def const(shape):
        return pl.BlockSpec(shape, lambda b, i, f: (0, 0))

    return pl.pallas_call(
        functools.partial(_ffn_kernel, tm=tm, final=final),
        grid=(B, S // tm, nf),
        in_specs=[
            pl.BlockSpec((None, H, D), lambda b, i, f: (b, jnp.maximum(i * nh - 1, 0), 0)),
            pl.BlockSpec((None, tm, D), lambda b, i, f: (b, i, 0)),
            pl.BlockSpec((None, H, D), lambda b, i, f: (b, jnp.minimum((i + 1) * nh, last), 0)),
            const((1, D)),
            pl.BlockSpec((D, fc), lambda b, i, f: (0, f)),
            pl.BlockSpec((D, fc), lambda b, i, f: (0, nf + f)),
            pl.BlockSpec((3, fc), lambda b, i, f: (0, f)),
            pl.BlockSpec((3, fc), lambda b, i, f: (0, nf + f)),
            pl.BlockSpec((1, fc), lambda b, i, f: (0, f)),
            pl.BlockSpec((1, fc), lambda b, i, f: (0, nf + f)),
            pl.BlockSpec((fc, D), lambda b, i, f: (f, 0)),
            const((1, D)),
        ],
        out_specs=pl.BlockSpec((None, tm, D), lambda b, i, f: (b, i, 0)),
        out_shape=jax.ShapeDtypeStruct((B, S, D), F32),
        scratch_shapes=[pltpu.VMEM((tm + 2 * H, D), BF16), pltpu.VMEM((tm, D), F32)],
        compiler_params=_cparams(("parallel", "parallel", "arbitrary")),
        name="conv_ffn",
    )(x, x, x, g, w_up, w_up, dw_w, dw_w, dw_b, dw_b, w_down, final_g)


def _trunk(x, mem, p, lbs):
    S = x.shape[1]
    rope = _rope_tables(S)
    for l in range(DEPTH):
        proj, gates = _inproj(x, p["norm_mix_g"][l], p["w_in"][l])
        c = _conv_branch(proj, p["conv_dw_w"][l], p["conv_dw_b"][l], p["conv_ln_g"][l], p["conv_ln_b"][l])
        a = _diff_attention(proj, rope, p["attn_lambda"][l], p["attn_subln_g"][l], l)
        o_f, o_b = _hgrn(proj, gates, lbs[l])
        kv = _mem_kv(mem, p["norm_mem_g"][l], p["w_ckv"][l])
        x = _merge_cross(x, proj, c, a, o_f, o_b, kv, p["w_conv_out"][l], p["w_attn_out"][l],
                         p["w_hg_out"][l], p["w_o"][l], p["w_cq"][l], p["w_co"][l],
                         p["hg_norm_g"][l], p["norm_cross_g"][l])
        x = _conv_ffn(x, p["norm_ffn_g"][l], p["w_up"][l], p["ffn_dw_w"][l], p["ffn_dw_b"][l],
                      p["w_down"][l], p["final_norm_g"], final=(l == DEPTH - 1))
    return x


def _prepare(norm_mix_g, w_in, conv_dw_w, conv_dw_b, conv_ln_g, conv_ln_b, w_conv_out, attn_lambda,
             attn_subln_g, w_attn_out, hg_lb_param, hg_norm_g, w_hg_out, w_o, norm_cross_g, norm_mem_g,
             w_cq, w_ckv, w_co, norm_ffn_g, w_up, ffn_dw_w, ffn_dw_b, w_down, final_norm_g):
    row = lambda t: t.astype(F32)[:, None, :]
    p = {
        "norm_mix_g": row(norm_mix_g), "w_in": w_in.astype(BF16),
        "conv_dw_w": conv_dw_w.astype(F32), "conv_dw_b": row(conv_dw_b),
        "conv_ln_g": row(conv_ln_g), "conv_ln_b": row(conv_ln_b), "w_conv_out": w_conv_out.astype(BF16),
        "attn_lambda": attn_lambda.astype(F32), "attn_subln_g": row(attn_subln_g),
        "w_attn_out": w_attn_out.astype(BF16),
        "hg_norm_g": row(hg_norm_g), "w_hg_out": w_hg_out.astype(BF16), "w_o": w_o.astype(BF16),
        "norm_cross_g": row(norm_cross_g), "norm_mem_g": row(norm_mem_g),
        "w_cq": w_cq.astype(BF16), "w_ckv": w_ckv.astype(BF16), "w_co": w_co.astype(BF16),
        "norm_ffn_g": row(norm_ffn_g), "w_up": w_up.astype(BF16),
        "ffn_dw_w": ffn_dw_w.astype(F32), "ffn_dw_b": row(ffn_dw_b), "w_down": w_down.astype(BF16),
        "final_norm_g": final_norm_g.astype(F32)[None, :],
    }
    lp = jax.nn.softmax(hg_lb_param.astype(F32), axis=0)
    lbs = jnp.cumsum(lp, axis=0) - lp[0:1]
    return p, lbs


def kernel(x_prompt, x_sample, mem_prompt, mem_sample, norm_mix_g, w_in, conv_dw_w, conv_dw_b, conv_ln_g, conv_ln_b, w_conv_out, attn_lambda, attn_subln_g, w_attn_out, hg_lb_param, hg_norm_g, w_hg_out, w_o, norm_cross_g, norm_mem_g, w_cq, w_ckv, w_co, norm_ffn_g, w_up, ffn_dw_w, ffn_dw_b, w_down, final_norm_g):
    p, lbs = _prepare(norm_mix_g, w_in, conv_dw_w, conv_dw_b, conv_ln_g, conv_ln_b, w_conv_out, attn_lambda,
                      attn_subln_g, w_attn_out, hg_lb_param, hg_norm_g, w_hg_out, w_o, norm_cross_g,
                      norm_mem_g, w_cq, w_ckv, w_co, norm_ffn_g, w_up, ffn_dw_w, ffn_dw_b, w_down,
                      final_norm_g)
    y_prompt = _trunk(x_prompt, mem_prompt, p, lbs)
    y_sample = _trunk(x_sample, mem_sample, p, lbs)
    return (y_prompt, y_sample)
```

```python
import functools
import math

import jax
import jax.numpy as jnp
from jax import lax
from jax.experimental import pallas as pl
from jax.experimental.pallas import tpu as pltpu

F32 = jnp.float32
BF16 = jnp.bfloat16

D_MODEL = 1024
DEPTH = 4
CONV_W = 512
CONV_K = 31
CONV_HALO = 16
DA_HEADS = 4
DA_HD = 64
ROT_DIM = 16
ROPE_THETA = 500000.0
ATTN_KEY_CHUNK = 512
HG_HEADS = 4
HG_DK = 128
HG_CHUNK = 64
HG_STEP_CHUNKS = 4
HG_SUB = 8
GATE_FLOOR = 1e-30
X_HEADS = 4
X_HD = 256
D_FF = 2816
FF_CHUNK = 1408
FFN_HALO = 8
N_IN = 8192
EPS = 1e-6

GATE_CHUNK_W = 1024
GATE_CHUNK = 3
COL_AQ, COL_AK, COL_AV, COL_HQ, COL_HI, COL_HG = range(6)
COL_GATE_1024 = 3
LOG2E = 1.4426950408889634

VMEM_LIMIT = 56 * 1024 * 1024


def _cparams(sem):
    return pltpu.CompilerParams(dimension_semantics=sem, vmem_limit_bytes=VMEM_LIMIT)


def _rms(x, g):
    ms = jnp.mean(x * x, axis=-1, keepdims=True)
    return x * lax.rsqrt(ms + EPS) * g


def _sigmoid(x):
    return 1.0 / (1.0 + jnp.exp(-x))


def _silu(x):
    return x * _sigmoid(x)


def _dot(a, b):
    return jnp.dot(a, b, preferred_element_type=F32)


def _dot_nt(a, b):
    return lax.dot_general(a, b, (((1,), (1,)), ((), ())), preferred_element_type=F32)


def _dot_tn(a, b):
    return lax.dot_general(a, b, (((0,), (0,)), ((), ())), preferred_element_type=F32)


def _inproj_kernel(xp_ref, x_ref, xn_ref, g_ref, w_ref, cw_ref, cb_ref, lg_ref, lb_ref,
                   o_ref, gate_ref, c_ref, xe_ref, ext_ref, sh_ref, *, tm, sub):
    i = pl.program_id(1)
    n = pl.num_programs(1)
    H = CONV_HALO
    SUBL = HG_SUB
    tn = GATE_CHUNK_W
    g = g_ref[...]
    xe_ref[0:H, :] = jnp.where(i > 0, _rms(xp_ref[...], g), 0.0).astype(BF16)
    xe_ref[H:H + tm, :] = _rms(x_ref[...], g).astype(BF16)
    xe_ref[H + tm:H + tm + H, :] = jnp.where(i < n - 1, _rms(xn_ref[...], g), 0.0).astype(BF16)

    y = _dot(xe_ref[...], w_ref[:, 0:tn])
    ext_ref[...] = y[:, :CONV_W] * _sigmoid(y[:, CONV_W:])
    rows_sh = sh_ref.shape[1]
    for r in range(SUBL):
        sh_ref[r] = ext_ref[pl.ds(r, rows_sh), :]

    def conv_rows(r):
        off = H - CONV_K // 2
        acc = jnp.zeros((sub, CONV_W), F32)
        for j in range(CONV_K):
            start = j + off
            acc = acc + cw_ref[j:j + 1, :] * sh_ref[start % SUBL, pl.ds(r * sub + start - start % SUBL, sub), :]
        acc = acc + cb_ref[...]
        mu = jnp.mean(acc, axis=-1, keepdims=True)
        d = acc - mu
        var = jnp.mean(d * d, axis=-1, keepdims=True)
        z = d * lax.rsqrt(var + EPS) * lg_ref[...] + lb_ref[...]
        c_ref[r * sub:(r + 1) * sub, :] = _silu(z).astype(c_ref.dtype)

    xn = xe_ref[H:H + tm, :]
    n_chunks = N_IN // tn
    pieces = tm // sub
    done = 0
    for j in range(1, n_chunks):
        y = _dot(xn, w_ref[:, j * tn:(j + 1) * tn])
        if j == GATE_CHUNK:
            gate_ref[...] = y
        else:
            jo = j - 1 - (j > GATE_CHUNK)
            o_ref[:, jo * tn:(jo + 1) * tn] = y.astype(o_ref.dtype)
        upto = (pieces * j) // (n_chunks - 1)
        for r in range(done, upto):
            conv_rows(r)
        done = upto


def _inproj(x, g, w, dw_w, dw_b, ln_g, ln_b, tm=512, sub=64):
    B, S, D = x.shape
    N = w.shape[1]
    H = CONV_HALO
    nh = tm // H
    last = S // H - 1
    n_out = N - 2 * GATE_CHUNK_W

    def const(shape):
        return pl.BlockSpec(shape, lambda b, i: (0, 0))

    def tok(width):
        return pl.BlockSpec((None, tm, width), lambda b, i: (b, i, 0))

    return pl.pallas_call(
        functools.partial(_inproj_kernel, tm=tm, sub=sub),
        grid=(B, S // tm),
        in_specs=[
            pl.BlockSpec((None, H, D), lambda b, i: (b, jnp.maximum(i * nh - 1, 0), 0)),
            tok(D),
            pl.BlockSpec((None, H, D), lambda b, i: (b, jnp.minimum((i + 1) * nh, last), 0)),
            const((1, D)),
            pl.BlockSpec((D, N), lambda b, i: (0, 0), pipeline_mode=pl.Buffered(1)),
            const((CONV_K, CONV_W)), const((1, CONV_W)), const((1, CONV_W)), const((1, CONV_W)),
        ],
        out_specs=[tok(n_out), tok(GATE_CHUNK_W), tok(CONV_W)],
        out_shape=[jax.ShapeDtypeStruct((B, S, n_out), BF16),
                   jax.ShapeDtypeStruct((B, S, GATE_CHUNK_W), F32),
                   jax.ShapeDtypeStruct((B, S, CONV_W), BF16)],
        scratch_shapes=[pltpu.VMEM((tm + 2 * H, D), BF16),
                        pltpu.VMEM((tm + 2 * H, CONV_W), F32),
                        pltpu.VMEM((HG_SUB, tm + 2 * H - HG_SUB, CONV_W), F32)],
        compiler_params=_cparams(("parallel", "parallel")),
        name="inproj_conv",
    )(x, x, x, g, w, dw_w, dw_b, ln_g, ln_b)


def _rope(x, c, s_up, s_dn):
    n = x.shape[-1]
    half = ROT_DIM // 2
    return x * c + pltpu.roll(x, n - half, 1) * s_up + pltpu.roll(x, half, 1) * s_dn


def _attn_kernel(q_ref, k_ref, v_ref, c_ref, su_ref, sd_ref, lam_ref, g_ref, o_ref, kb_ref, vx_ref, qq_ref,
                 sa_ref, sb_ref, *, tq, lam_init):
    S = k_ref.shape[0]
    W = 2 * DA_HD
    nt = S // tq
    kb_ref[...] = _rope(k_ref[...].astype(F32), c_ref[...], su_ref[...], sd_ref[...]).astype(BF16)
    vx_ref[:, :W] = v_ref[...]
    vx_ref[:, W:] = jnp.where(lax.broadcasted_iota(jnp.int32, (S, W), 1) == 0, 1.0, 0.0).astype(BF16)

    lp = lam_ref[...]
    lam = (jnp.exp(jnp.sum(lp[0:1] * lp[1:2], axis=-1, keepdims=True))
           - jnp.exp(jnp.sum(lp[2:3] * lp[3:4], axis=-1, keepdims=True)) + lam_init)
    scale = DA_HD ** -0.5 * LOG2E

    def rows_of(t):
        return pl.ds(pl.multiple_of(t * tq, tq), tq)

    def score_chunk(qq, c, s_ref, mx):
        keys = slice(c * ATTN_KEY_CHUNK, (c + 1) * ATTN_KEY_CHUNK)
        s = _dot_nt(qq, kb_ref[keys, :])
        s_ref[:, keys] = s
        for i in range(ATTN_KEY_CHUNK // W):
            mx = jnp.maximum(mx, s[:, i * W:(i + 1) * W])
        return mx

    def write_out(t, r):
        inv = 1.0 / r[:, W:W + 1]
        o = r[:tq, :W] * inv[:tq] - (lam * inv[tq:]) * r[tq:, :W]
        o_ref[rows_of(t), :] = (_rms(o, g_ref[...]) * (1.0 - lam_init)).astype(o_ref.dtype)

    def step(t_next, nxt_ref, cur_ref, m_cur, t_prev, r_prev):
        qq = qq_ref[t_next]
        mx = jnp.full((2 * tq, W), -jnp.inf, F32)
        r = jnp.zeros((2 * tq, 2 * W), F32)
        for c in range(S // ATTN_KEY_CHUNK):
            keys = slice(c * ATTN_KEY_CHUNK, (c + 1) * ATTN_KEY_CHUNK)
            mx = score_chunk(qq, c, nxt_ref, mx)
            p = jnp.exp2(cur_ref[:, keys] - m_cur).astype(BF16)
            r = r + _dot(p, vx_ref[keys, :])
            if c == 0:
                write_out(t_prev, r_prev)
        return jnp.max(mx, axis=-1, keepdims=True), r

    q_all = _rope(q_ref[...].astype(F32), c_ref[...], su_ref[...], sd_ref[...]) * scale
    lane_all = lax.broadcasted_iota(jnp.int32, (S, W), 1) < DA_HD
    q_one = jnp.where(lane_all, q_all, 0.0).astype(BF16)
    q_two = jnp.where(lane_all, 0.0, q_all).astype(BF16)
    for t in range(nt):
        qq_ref[t, :tq, :] = q_one[t * tq:(t + 1) * tq]
        qq_ref[t, tq:, :] = q_two[t * tq:(t + 1) * tq]

    qq0 = qq_ref[0]
    mx0 = jnp.full((2 * tq, W), -jnp.inf, F32)
    for c in range(S // ATTN_KEY_CHUNK):
        mx0 = score_chunk(qq0, c, sa_ref, mx0)
    m0 = jnp.max(mx0, axis=-1, keepdims=True)

    def body(j, carry):
        m_a, r_prev = carry
        t0 = 2 * j
        m_b, r_a = step(t0 + 1, sb_ref, sa_ref, m_a, jnp.maximum(t0 - 1, 0), r_prev)
        m_a, r_b = step(jnp.minimum(t0 + 2, nt - 1), sa_ref, sb_ref, m_b, t0, r_a)
        return m_a, r_b

    _, r_last = lax.fori_loop(0, nt // 2, body, (m0, jnp.ones((2 * tq, 2 * W), F32)))
    write_out(nt - 1, r_last)


def _rope_tables(S):
    half = ROT_DIM // 2
    inv = 1.0 / (ROPE_THETA ** (jnp.arange(0, ROT_DIM, 2, dtype=F32) / ROT_DIM))
    ang = jnp.arange(S, dtype=F32)[:, None] * inv[None, :]
    cos, sin = jnp.cos(ang), jnp.sin(ang)
    pad = jnp.zeros((S, DA_HD - ROT_DIM), F32)
    zero = jnp.zeros((S, half), F32)
    c = jnp.concatenate([cos, cos, pad + 1.0], axis=-1)
    s_up = jnp.concatenate([-sin, zero, pad], axis=-1)
    s_dn = jnp.concatenate([zero, sin, pad], axis=-1)
    return tuple(jnp.tile(t, (1, 2)) for t in (c, s_up, s_dn))


def _diff_attention(proj, rope, lam_params, subln_g, layer, tq=256):
    B, S, _ = proj.shape
    W = 2 * DA_HD
    lam_init = 0.8 - 0.6 * math.exp(-0.3 * layer)

    def head(col):
        return pl.BlockSpec((None, S, W), lambda b, h: (b, 0, col * (CONV_W // W) + h))

    def const(shape):
        return pl.BlockSpec(shape, lambda b, h: (0, 0))

    return pl.pallas_call(
        functools.partial(_attn_kernel, tq=tq, lam_init=lam_init),
        grid=(B, DA_HEADS),
        in_specs=[head(COL_AQ), head(COL_AK), head(COL_AV), const((S, W)), const((S, W)), const((S, W)),
                  const((4, DA_HD)), const((1, W))],
        out_specs=pl.BlockSpec((None, S, W), lambda b, h: (b, 0, h)),
        out_shape=jax.ShapeDtypeStruct((B, S, DA_HEADS * W), BF16),
        scratch_shapes=[pltpu.VMEM((S, W), BF16), pltpu.VMEM((S, 2 * W), BF16),
                        pltpu.VMEM((S // tq, 2 * tq, W), BF16),
                        pltpu.VMEM((2 * tq, S), F32), pltpu.VMEM((2 * tq, S), F32)],
        compiler_params=_cparams(("parallel", "parallel")),
        name="diff_attention",
    )(proj, proj, proj, *rope, lam_params, subln_g)


def _split3(x):
    hi = x.astype(BF16)
    r1 = x - hi.astype(F32)
    mid = r1.astype(BF16)
    lo = (r1 - mid.astype(F32)).astype(BF16)
    return hi, mid, lo


def _boundary_rows(G, b, rev):
    C, K = G.shape
    e = b if rev else b - 1
    if 2 * b >= HG_SUB:
        G3 = G.reshape(C // (2 * b), 2 * b, K)
        return jnp.broadcast_to(G3[:, e:e + 1, :], G3.shape).reshape(C, K)
    assert b == 2
    G3 = G.reshape(C // HG_SUB, HG_SUB, K)
    lo = jnp.broadcast_to(G3[:, e:e + 1, :], G3.shape)
    hi = jnp.broadcast_to(G3[:, 4 + e:5 + e, :], G3.shape)
    row = lax.broadcasted_iota(jnp.int32, G3.shape, 1)
    return jnp.where(row < 4, lo, hi).reshape(C, K)


def _hgrn_masks(rev):
    C = HG_CHUNK
    r_i = lax.broadcasted_iota(jnp.int32, (C, C), 0)
    c_i = lax.broadcasted_iota(jnp.int32, (C, C), 1)
    row_k = lax.broadcasted_iota(jnp.int32, (C, HG_DK), 0)
    tri = jnp.where((c_i >= r_i) if rev else (c_i <= r_i), 1.0, 0.0).astype(BF16)
    levels = []
    b = C // 2
    while b >= 1:
        t_later = (r_i // b) % 2 == (0 if rev else 1)
        sibling = (c_i // b) == (r_i // b) + (1 if rev else -1)
        q_rows = (row_k // b) % 2 == (0 if rev else 1)
        levels.append((b, t_later & sibling, jnp.where(q_rows, 1.0, -1.0)))
        b //= 2
    return tri, levels


def _hgrn_prep(q_raw, f_raw, v_raw, lb, tri):
    forget = lb + (1.0 - lb) * _sigmoid(f_raw)
    decay = jnp.maximum(forget, GATE_FLOOR)
    hi, mid, lo = _split3(jnp.log(decay))
    G2 = (_dot(tri, hi) + _dot(tri, mid) + _dot(tri, lo)) * LOG2E
    return dict(G2=G2, q=_silu(q_raw.astype(F32)), k=1.0 - forget, v=v_raw, decay=decay)


def _hgrn_scores(t, levels, rev):
    C = HG_CHUNK
    out = []
    for h in range(HG_HEADS):
        sl = slice(h * HG_DK, (h + 1) * HG_DK)
        G2, q, k, decay = t["G2"][:, sl], t["q"][:, sl], t["k"][:, sl], t["decay"][:, sl]
        qb, kb = q.astype(BF16), k.astype(BF16)
        a = jnp.zeros((C, C), F32)
        for b, pair_mask, sign in levels:
            if b == 1:
                qt, kt = (q * decay).astype(BF16), kb
            else:
                d = G2 - _boundary_rows(G2, b, rev)
                e = jnp.exp2(d * sign).astype(BF16)
                qt, kt = qb * e, kb * e
            a = a + jnp.where(pair_mask, _dot_nt(qt, kt), 0.0)
        out.append(a.astype(BF16))
    return out


def _hgrn_finish(t, scores, states, rev):
    C = HG_CHUNK
    outs, new_states = [], []
    for h in range(HG_HEADS):
        sl = slice(h * HG_DK, (h + 1) * HG_DK)
        G2, q, k, v = t["G2"][:, sl], t["q"][:, sl], t["k"][:, sl], t["v"][:, sl]
        G2_end = G2[0:1] if rev else G2[C - 1:C]
        st = states[h]
        inter = _dot_nt((q * jnp.exp2(G2)).astype(BF16), st.astype(BF16))
        k_dec = k * jnp.exp2(G2_end - G2)
        new_states.append(st * jnp.exp2(G2_end) + _dot_tn(v, k_dec.astype(BF16)))
        intra = _dot(scores[h], v)
        diag = jnp.sum(q * k, axis=-1, keepdims=True) * v.astype(F32)
        outs.append(inter + intra + diag)
    return jnp.concatenate(outs, axis=-1), new_states


def _hgrn_kernel(qf_ref, ff_ref, vf_ref, qb_ref, fb_ref, vb_ref, lb_ref, of_ref, ob_ref, stf_ref, stb_ref):
    @pl.when(pl.program_id(1) == 0)
    def _():
        stf_ref[...] = jnp.zeros_like(stf_ref)
        stb_ref[...] = jnp.zeros_like(stb_ref)

    C = HG_CHUNK
    dirs = [dict(rev=False, q=qf_ref, f=ff_ref, v=vf_ref, lb=lb_ref[0:1, :], o=of_ref, st=stf_ref,
                 order=range(HG_STEP_CHUNKS)),
            dict(rev=True, q=qb_ref, f=fb_ref, v=vb_ref, lb=lb_ref[1:2, :], o=ob_ref, st=stb_ref,
                 order=range(HG_STEP_CHUNKS - 1, -1, -1))]
    for d in dirs:
        d["tri"], d["levels"] = _hgrn_masks(d["rev"])
        d["rows"] = [slice(j * C, (j + 1) * C) for j in d["order"]]
        d["chunks"] = [_hgrn_prep(d["q"][r, :], d["f"][r, :], d["v"][r, :], d["lb"], d["tri"])
                       for r in d["rows"]]
    for d in dirs:
        d["scores"] = [_hgrn_scores(t, d["levels"], d["rev"]) for t in d["chunks"]]
    for d in dirs:
        d["states"] = [d["st"][h] for h in range(HG_HEADS)]
    for j in range(HG_STEP_CHUNKS):
        for d in dirs:
            d["o"][d["rows"][j], :], d["states"] = _hgrn_finish(d["chunks"][j], d["scores"][j], d["states"],
                                                                d["rev"])
    for d in dirs:
        for h in range(HG_HEADS):
            d["st"][h] = d["states"][h]


def _hgrn(proj, gates, lb):
    B, S, _ = proj.shape
    C = HG_CHUNK * HG_STEP_CHUNKS
    n = S // C
    W = HG_HEADS * HG_DK

    def fwd(col):
        return pl.BlockSpec((None, C, W), lambda b, c: (b, c, col))

    def bwd(col):
        return pl.BlockSpec((None, C, W), lambda b, c: (b, n - 1 - c, col))

    return pl.pallas_call(
        _hgrn_kernel,
        grid=(B, n),
        in_specs=[fwd(COL_HQ), fwd(0), fwd(COL_HI), bwd(COL_HQ), bwd(1), bwd(COL_HI),
                  pl.BlockSpec((2, W), lambda b, c: (0, 0))],
        out_specs=[pl.BlockSpec((None, C, W), lambda b, c: (b, c, 0)),
                   pl.BlockSpec((None, C, W), lambda b, c: (b, n - 1 - c, 0))],
        out_shape=[jax.ShapeDtypeStruct((B, S, W), F32), jax.ShapeDtypeStruct((B, S, W), F32)],
        scratch_shapes=[pltpu.VMEM((HG_HEADS, HG_DK, HG_DK), F32), pltpu.VMEM((HG_HEADS, HG_DK, HG_DK), F32)],
        compiler_params=_cparams(("parallel", "arbitrary")),
        name="hgrn2",
    )(proj, gates, proj, proj, gates, proj, lb)


def _memkv_kernel(m_ref, g_ref, w_ref, o_ref):
    o_ref[...] = _dot(_rms(m_ref[...], g_ref[...]).astype(BF16), w_ref[...]).astype(o_ref.dtype)


def _mem_kv(mem, g, w):
    B, M, D = mem.shape
    N = w.shape[1]
    return pl.pallas_call(
        _memkv_kernel,
        grid=(B,),
        in_specs=[pl.BlockSpec((None, M, D), lambda b: (b, 0, 0)),
                  pl.BlockSpec((1, D), lambda b: (0, 0)),
                  pl.BlockSpec((D, N), lambda b: (0, 0))],
        out_specs=pl.BlockSpec((None, M, N), lambda b: (b, 0, 0)),
        out_shape=jax.ShapeDtypeStruct((B, M, N), BF16),
        compiler_params=_cparams(("parallel",)),
        name="mem_kv",
    )(mem, g, w)


def _merge_kernel(x_ref, c_ref, a_ref, of_ref, ob_ref, hg_ref, g0_ref, g1_ref, g2_ref, kv_ref,
                  wc_ref, wa_ref, wh_ref, wo_ref, wq_ref, wco_ref, hgn_ref, ncg_ref, o_ref, *, tm, parts):
    rows = [slice(r * (tm // parts), (r + 1) * (tm // parts)) for r in range(parts)]

    def branches(r):
        o_sum = of_ref[r, :] + ob_ref[r, :]
        gate = _silu(hg_ref[r, :].astype(F32))
        heads = []
        for h in range(HG_HEADS):
            sl = slice(h * HG_DK, (h + 1) * HG_DK)
            heads.append(_rms(o_sum[:, sl], hgn_ref[...]) * gate[:, sl])
        oh = jnp.concatenate(heads, axis=-1).astype(BF16)
        return _dot(c_ref[r, :], wc_ref[...]), _dot(a_ref[r, :], wa_ref[...]), _dot(oh, wh_ref[...])

    def mix(r, ys):
        y_c, y_a, y_h = ys
        merged = (_sigmoid(g0_ref[r, :].astype(F32)) * y_c + _sigmoid(g1_ref[r, :].astype(F32)) * y_a
                  + _sigmoid(g2_ref[r, :].astype(F32)) * y_h)
        return x_ref[r, :] + _dot(merged.astype(BF16), wo_ref[...])

    def query(x1):
        hn = _rms(x1, ncg_ref[...]).astype(BF16)
        return (_dot(hn, wq_ref[...]) * (X_HD ** -0.5 * LOG2E)).astype(BF16)

    def attend(q):
        outs = []
        for h in range(X_HEADS):
            k = kv_ref[:, h * X_HD:(h + 1) * X_HD]
            v = kv_ref[:, D_MODEL + h * X_HD:D_MODEL + (h + 1) * X_HD]
            s = _dot_nt(q[:, h * X_HD:(h + 1) * X_HD], k)
            p = jnp.exp2(s - jnp.max(s, axis=-1, keepdims=True))
            p = p * (1.0 / jnp.sum(p, axis=-1, keepdims=True))
            outs.append(_dot(p.astype(BF16), v).astype(BF16))
        return jnp.concatenate(outs, axis=-1)

    ys = [branches(r) for r in rows]
    x1 = [mix(r, y) for r, y in zip(rows, ys)]
    q = [query(x) for x in x1]
    o = [attend(t) for t in q]
    for r, x, t in zip(rows, x1, o):
        o_ref[r, :] = x + _dot(t, wco_ref[...])


def _merge_cross(x, proj, c, a, o_f, o_b, kv, wc, wa, wh, wo, wq, wco, hgn, ncg, tm=512, parts=2):
    B, S, D = x.shape
    M = kv.shape[1]
    W = CONV_W

    def tok(width, col=0):
        return pl.BlockSpec((None, tm, width), lambda b, i: (b, i, col))

    def const(shape):
        return pl.BlockSpec(shape, lambda b, i: (0, 0), pipeline_mode=pl.Buffered(1))

    return pl.pallas_call(
        functools.partial(_merge_kernel, tm=tm, parts=parts),
        grid=(B, S // tm),
        in_specs=[tok(D), tok(W), tok(W), tok(W), tok(W), tok(W, COL_HG),
                  tok(D, COL_GATE_1024), tok(D, COL_GATE_1024 + 1), tok(D, COL_GATE_1024 + 2),
                  pl.BlockSpec((None, M, 2 * D), lambda b, i: (b, 0, 0)),
                  const((W, D)), const((W, D)), const((W, D)), const((D, D)), const((D, D)), const((D, D)),
                  const((1, HG_DK)), const((1, D))],
        out_specs=tok(D),
        out_shape=jax.ShapeDtypeStruct((B, S, D), F32),
        compiler_params=_cparams(("parallel", "parallel")),
        name="merge_cross",
    )(x, c, a, o_f, o_b, proj, proj, proj, proj, kv, wc, wa, wh, wo, wq, wco, hgn, ncg)


def _ffn_kernel(xp_ref, x_ref, xn_ref, g_ref, wa_ref, wb_ref, dwa_ref, dwb_ref, ba_ref, bb_ref, wd_ref,
                fg_ref, o_ref, hn_ref, acc_ref, *, tm, final):
    i = pl.program_id(1)
    n = pl.num_programs(1)
    f = pl.program_id(2)
    nf = pl.num_programs(2)
    H = FFN_HALO

    @pl.when(f == 0)
    def _():
        g = g_ref[...]
        hn_ref[0:H, :] = jnp.where(i > 0, _rms(xp_ref[...], g), 0.0).astype(BF16)
        hn_ref[H:H + tm, :] = _rms(x_ref[...], g).astype(BF16)
        hn_ref[H + tm:H + tm + H, :] = jnp.where(i < n - 1, _rms(xn_ref[...], g), 0.0).astype(BF16)
        acc_ref[...] = jnp.zeros_like(acc_ref)

    hn = hn_ref[...]
    rows = tm + 2 * H

    def conv(u, dw_ref, b_ref):
        up = pltpu.roll(u, 1, 0)[H:H + tm]
        dn = pltpu.roll(u, rows - 1, 0)[H:H + tm]
        return dw_ref[0:1, :] * up + dw_ref[1:2, :] * u[H:H + tm] + dw_ref[2:3, :] * dn + b_ref[...]

    a = conv(_dot(hn, wa_ref[...]), dwa_ref, ba_ref)
    b = conv(_dot(hn, wb_ref[...]), dwb_ref, bb_ref)
    acc_ref[...] += _dot((_silu(a) * b).astype(BF16), wd_ref[...])

    @pl.when(f == nf - 1)
    def _():
        y = x_ref[...] + acc_ref[...]
        if final:
            y = _rms(y, fg_ref[...])
        o_ref[...] = y


def _conv_ffn(x, g, w_up, dw_w, dw_b, w_down, final_g, final, tm=512):
    B, S, D = x.shape
    H = FFN_HALO
    nh = tm // H
    last = S // H - 1
    fc = FF_CHUNK
    nf = D_FF // fc

    def const(shape):
        return pl.BlockSpec(shape, lambda b, i, f: (0, 0))

    return pl.pallas_call(
        functools.partial(_ffn_kernel, tm=tm, final=final),
        grid=(B, S // tm, nf),
        in_specs=[
            pl.BlockSpec((None, H, D), lambda b, i, f: (b, jnp.maximum(i * nh - 1, 0), 0)),
            pl.BlockSpec((None, tm, D), lambda b, i, f: (b, i, 0)),
            pl.BlockSpec((None, H, D), lambda b, i, f: (b, jnp.minimum((i + 1) * nh, last), 0)),
            const((1, D)),
            pl.BlockSpec((D, fc), lambda b, i, f: (0, f)),
            pl.BlockSpec((D, fc), lambda b, i, f: (0, nf + f)),
            pl.BlockSpec((3, fc), lambda b, i, f: (0, f)),
            pl.BlockSpec((3, fc), lambda b, i, f: (0, nf + f)),
            pl.BlockSpec((1, fc), lambda b, i, f: (0, f)),
            pl.BlockSpec((1, fc), lambda b, i, f: (0, nf + f)),
            pl.BlockSpec((fc, D), lambda b, i, f: (f, 0)),
            const((1, D)),
        ],
        out_specs=pl.BlockSpec((None, tm, D), lambda b, i, f: (b, i, 0)),
        out_shape=jax.ShapeDtypeStruct((B, S, D), F32),
        scratch_shapes=[pltpu.VMEM((tm + 2 * H, D), BF16), pltpu.VMEM((tm, D), F32)],
        compiler_params=_cparams(("parallel", "parallel", "arbitrary")),
        name="conv_ffn",
    )(x, x, x, g, w_up, w_up, dw_w, dw_w, dw_b, dw_b, w_down, final_g)


def _trunk(x, mem, p, lbs):
    S = x.shape[1]
    rope = _rope_tables(S)
    for l in range(DEPTH):
        proj, gates, c = _inproj(x, p["norm_mix_g"][l], p["w_in"][l], p["conv_dw_w"][l], p["conv_dw_b"][l],
                                 p["conv_ln_g"][l], p["conv_ln_b"][l])
        a = _diff_attention(proj, rope, p["attn_lambda"][l], p["attn_subln_g"][l], l)
        o_f, o_b = _hgrn(proj, gates, lbs[l])
        kv = _mem_kv(mem, p["norm_mem_g"][l], p["w_ckv"][l])
        x = _merge_cross(x, proj, c, a, o_f, o_b, kv, p["w_conv_out"][l], p["w_attn_out"][l],
                         p["w_hg_out"][l], p["w_o"][l], p["w_cq"][l], p["w_co"][l],
                         p["hg_norm_g"][l], p["norm_cross_g"][l])
        x = _conv_ffn(x, p["norm_ffn_g"][l], p["w_up"][l], p["ffn_dw_w"][l], p["ffn_dw_b"][l],
                      p["w_down"][l], p["final_norm_g"], final=(l == DEPTH - 1))
    return x


def _prepare(norm_mix_g, w_in, conv_dw_w, conv_dw_b, conv_ln_g, conv_ln_b, w_conv_out, attn_lambda,
             attn_subln_g, w_attn_out, hg_lb_param, hg_norm_g, w_hg_out, w_o, norm_cross_g, norm_mem_g,
             w_cq, w_ckv, w_co, norm_ffn_g, w_up, ffn_dw_w, ffn_dw_b, w_down, final_norm_g):
    row = lambda t: t.astype(F32)[:, None, :]
    p = {
        "norm_mix_g": row(norm_mix_g), "w_in": w_in.astype(BF16),
        "conv_dw_w": conv_dw_w.astype(F32), "conv_dw_b": row(conv_dw_b),
        "conv_ln_g": row(conv_ln_g), "conv_ln_b": row(conv_ln_b), "w_conv_out": w_conv_out.astype(BF16),
        "attn_lambda": attn_lambda.astype(F32), "attn_subln_g": row(attn_subln_g),
        "w_attn_out": w_attn_out.astype(BF16),
        "hg_norm_g": row(hg_norm_g), "w_hg_out": w_hg_out.astype(BF16), "w_o": w_o.astype(BF16),
        "norm_cross_g": row(norm_cross_g), "norm_mem_g": row(norm_mem_g),
        "w_cq": w_cq.astype(BF16), "w_ckv": w_ckv.astype(BF16), "w_co": w_co.astype(BF16),
        "norm_ffn_g": row(norm_ffn_g), "w_up": w_up.astype(BF16),
        "ffn_dw_w": ffn_dw_w.astype(F32), "ffn_dw_b": row(ffn_dw_b), "w_down": w_down.astype(BF16),
        "final_norm_g": final_norm_g.astype(F32)[None, :],
    }
    lp = jax.nn.softmax(hg_lb_param.astype(F32), axis=0)
    lbs = jnp.cumsum(lp, axis=0) - lp[0:1]
    return p, lbs


def kernel(x_prompt, x_sample, mem_prompt, mem_sample, norm_mix_g, w_in, conv_dw_w, conv_dw_b, conv_ln_g, conv_ln_b, w_conv_out, attn_lambda, attn_subln_g, w_attn_out, hg_lb_param, hg_norm_g, w_hg_out, w_o, norm_cross_g, norm_mem_g, w_cq, w_ckv, w_co, norm_ffn_g, w_up, ffn_dw_w, ffn_dw_b, w_down, final_norm_g):
    p, lbs = _prepare(norm_mix_g, w_in, conv_dw_w, conv_dw_b, conv_ln_g, conv_ln_b, w_conv_out, attn_lambda,
                      attn_subln_g, w_attn_out, hg_lb_param, hg_norm_g, w_hg_out, w_o, norm_cross_g,
                      norm_mem_g, w_cq, w_ckv, w_co, norm_ffn_g, w_up, ffn_dw_w, ffn_dw_b, w_down,
                      final_norm_g)
    y_prompt = _trunk(x_prompt, mem_prompt, p, lbs)
    y_sample = _trunk(x_sample, mem_sample, p, lbs)
    return (y_prompt, y_sample)
```

```python
import functools
import math

import jax
import jax.numpy as jnp
from jax import lax
from jax.experimental import pallas as pl
from jax.experimental.pallas import tpu as pltpu

F32 = jnp.float32
BF16 = jnp.bfloat16

D_MODEL = 1024
DEPTH = 4
CONV_W = 512
CONV_K = 31
CONV_HALO = 16
DA_HEADS = 4
DA_HD = 64
ROT_DIM = 16
ROPE_THETA = 500000.0
ATTN_KEY_CHUNK = 512
HG_HEADS = 4
HG_DK = 128
HG_CHUNK = 64
HG_STEP_CHUNKS = 4
HG_SUB = 8
GATE_FLOOR = 1e-30
X_HEADS = 4
X_HD = 256
D_FF = 2816
FF_CHUNK = 1408
FFN_HALO = 8
N_IN = 8192
EPS = 1e-6

GATE_CHUNK_W = 1024
GATE_CHUNK = 3
ROPE_CHUNK = 1
COL_AQ, COL_AK, COL_AV, COL_HQ, COL_HI, COL_HG = range(6)
COL_GATE_1024 = 3
LOG2E = 1.4426950408889634
ATTN_Q_SCALE = DA_HD ** -0.5 * LOG2E

VMEM_LIMIT = 56 * 1024 * 1024


def _cparams(sem):
    return pltpu.CompilerParams(dimension_semantics=sem, vmem_limit_bytes=VMEM_LIMIT)


def _rms(x, g):
    ms = jnp.mean(x * x, axis=-1, keepdims=True)
    return x * lax.rsqrt(ms + EPS) * g


def _sigmoid(x):
    return 1.0 / (1.0 + jnp.exp(-x))


def _silu(x):
    return x * _sigmoid(x)


def _dot(a, b):
    return jnp.dot(a, b, preferred_element_type=F32)


def _dot_nt(a, b):
    return lax.dot_general(a, b, (((1,), (1,)), ((), ())), preferred_element_type=F32)


def _dot_tn(a, b):
    return lax.dot_general(a, b, (((0,), (0,)), ((), ())), preferred_element_type=F32)


def _rope(x, c, s_up, s_dn):
    n = x.shape[-1]
    half = ROT_DIM // 2
    return x * c + pltpu.roll(x, n - half, 1) * s_up + pltpu.roll(x, half, 1) * s_dn


def _inproj_kernel(xp_ref, x_ref, xn_ref, g_ref, w_ref, cw_ref, cb_ref, lg_ref, lb_ref, rc_ref, rsu_ref, rsd_ref,
                   o_ref, gate_ref, c_ref, xe_ref, ext_ref, sh_ref, cwb_ref, *, tm, sub):
    i = pl.program_id(1)
    n = pl.num_programs(1)
    H = CONV_HALO
    SUBL = HG_SUB
    tn = GATE_CHUNK_W
    g = g_ref[...]
    xe_ref[0:H, :] = jnp.where(i > 0, _rms(xp_ref[...], g), 0.0).astype(BF16)
    xe_ref[H:H + tm, :] = _rms(x_ref[...], g).astype(BF16)
    xe_ref[H + tm:H + tm + H, :] = jnp.where(i < n - 1, _rms(xn_ref[...], g), 0.0).astype(BF16)

    y = _dot(xe_ref[...], w_ref[:, 0:tn])
    ext_ref[...] = y[:, :CONV_W] * _sigmoid(y[:, CONV_W:])
    rows_sh = sh_ref.shape[1]
    for r in range(SUBL):
        sh_ref[r] = ext_ref[pl.ds(r, rows_sh), :]

    for j in range(CONV_K):
        cwb_ref[j] = jnp.broadcast_to(cw_ref[j:j + 1, :], (SUBL, CONV_W))

    def conv_rows(r):
        off = H - CONV_K // 2
        acc = jnp.zeros((sub // SUBL, SUBL, CONV_W), F32)
        for j in range(CONV_K):
            start = j + off
            taps = sh_ref[start % SUBL, pl.ds(r * sub + start - start % SUBL, sub), :]
            acc = acc + cwb_ref[j][None] * taps.reshape(sub // SUBL, SUBL, CONV_W)
        acc = acc.reshape(sub, CONV_W) + cb_ref[...]
        mu = jnp.mean(acc, axis=-1, keepdims=True)
        d = acc - mu
        var = jnp.mean(d * d, axis=-1, keepdims=True)
        z = d * lax.rsqrt(var + EPS) * lg_ref[...] + lb_ref[...]
        c_ref[r * sub:(r + 1) * sub, :] = _silu(z).astype(c_ref.dtype)

    xn = xe_ref[H:H + tm, :]
    n_chunks = N_IN // tn
    pieces = tm // sub
    done = 0
    for j in range(1, n_chunks):
        y = _dot(xn, w_ref[:, j * tn:(j + 1) * tn])
        jo = j - 1 - (j > GATE_CHUNK)
        if j == GATE_CHUNK:
            gate_ref[...] = y
        elif j == ROPE_CHUNK:
            W = 2 * DA_HD
            for h in range(tn // W):
                r = _rope(y[:, h * W:(h + 1) * W], rc_ref[...], rsu_ref[...], rsd_ref[...])
                if h < DA_HEADS:
                    r = r * ATTN_Q_SCALE
                o_ref[:, jo * tn + h * W:jo * tn + (h + 1) * W] = r.astype(o_ref.dtype)
        else:
            o_ref[:, jo * tn:(jo + 1) * tn] = y.astype(o_ref.dtype)
        upto = (pieces * j) // (n_chunks - 1)
        for r in range(done, upto):
            conv_rows(r)
        done = upto


def _inproj(x, g, w, dw_w, dw_b, ln_g, ln_b, rope, tm=512, sub=16):
    B, S, D = x.shape
    N = w.shape[1]
    H = CONV_HALO
    nh = tm // H
    last = S // H - 1
    n_out = N - 2 * GATE_CHUNK_W

    def const(shape):
        return pl.BlockSpec(shape, lambda b, i: (0, 0))

    def tok(width):
        return pl.BlockSpec((None, tm, width), lambda b, i: (b, i, 0))

    rope_rows = pl.BlockSpec((tm, 2 * DA_HD), lambda b, i: (i, 0))

    return pl.pallas_call(
        functools.partial(_inproj_kernel, tm=tm, sub=sub),
        grid=(B, S // tm),
        in_specs=[
            pl.BlockSpec((None, H, D), lambda b, i: (b, jnp.maximum(i * nh - 1, 0), 0)),
            tok(D),
            pl.BlockSpec((None, H, D), lambda b, i: (b, jnp.minimum((i + 1) * nh, last), 0)),
            const((1, D)),
            pl.BlockSpec((D, N), lambda b, i: (0, 0), pipeline_mode=pl.Buffered(1)),
            const((CONV_K, CONV_W)), const((1, CONV_W)), const((1, CONV_W)), const((1, CONV_W)),
            rope_rows, rope_rows, rope_rows,
        ],
        out_specs=[tok(n_out), tok(GATE_CHUNK_W), tok(CONV_W)],
        out_shape=[jax.ShapeDtypeStruct((B, S, n_out), BF16),
                   jax.ShapeDtypeStruct((B, S, GATE_CHUNK_W), F32),
                   jax.ShapeDtypeStruct((B, S, CONV_W), BF16)],
        scratch_shapes=[pltpu.VMEM((tm + 2 * H, D), BF16),
                        pltpu.VMEM((tm + 2 * H, CONV_W), F32),
                        pltpu.VMEM((HG_SUB, tm + 2 * H - HG_SUB, CONV_W), F32),
                        pltpu.VMEM((CONV_K, HG_SUB, CONV_W), F32)],
        compiler_params=_cparams(("parallel", "parallel")),
        name="inproj_conv",
    )(x, x, x, g, w, dw_w, dw_b, ln_g, ln_b, *rope)


def _attn_kernel(q_ref, k_ref, v_ref, lam_ref, g_ref, o_ref, vx_ref, qq_ref, sa_ref, sb_ref, *, tq, lam_init):
    S = k_ref.shape[0]
    W = 2 * DA_HD
    nt = S // tq
    vx_ref[:, :W] = v_ref[...]
    vx_ref[:, W:] = jnp.where(lax.broadcasted_iota(jnp.int32, (S, W), 1) == 0, 1.0, 0.0).astype(BF16)

    lp = lam_ref[...]
    lam = (jnp.exp(jnp.sum(lp[0:1] * lp[1:2], axis=-1, keepdims=True))
           - jnp.exp(jnp.sum(lp[2:3] * lp[3:4], axis=-1, keepdims=True)) + lam_init)

    def rows_of(t):
        return pl.ds(pl.multiple_of(t * tq, tq), tq)

    def score_chunk(qq, c, s_ref, mx):
        keys = slice(c * ATTN_KEY_CHUNK, (c + 1) * ATTN_KEY_CHUNK)
        s = _dot_nt(qq, k_ref[keys, :])
        s_ref[:, keys] = s
        for i in range(ATTN_KEY_CHUNK // W):
            mx = jnp.maximum(mx, s[:, i * W:(i + 1) * W])
        return mx

    def write_out(t, r):
        inv = 1.0 / r[:, W:W + 1]
        o = r[:tq, :W] * inv[:tq] - (lam * inv[tq:]) * r[tq:, :W]
        o_ref[rows_of(t), :] = (_rms(o, g_ref[...]) * (1.0 - lam_init)).astype(o_ref.dtype)

    def step(t_next, nxt_ref, cur_ref, m_cur, t_prev, r_prev):
        qq = qq_ref[t_next]
        mx = jnp.full((2 * tq, W), -jnp.inf, F32)
        r = jnp.zeros((2 * tq, 2 * W), F32)
        for c in range(S // ATTN_KEY_CHUNK):
            keys = slice(c * ATTN_KEY_CHUNK, (c + 1) * ATTN_KEY_CHUNK)
            mx = score_chunk(qq, c, nxt_ref, mx)
            p = jnp.exp2(cur_ref[:, keys] - m_cur).astype(BF16)
            r = r + _dot(p, vx_ref[keys, :])
            if c == 0:
                write_out(t_prev, r_prev)
        return jnp.max(mx, axis=-1, keepdims=True), r

    q_all = q_ref[...]
    lane_all = lax.broadcasted_iota(jnp.int32, (S, W), 1) < DA_HD
    q_one = jnp.where(lane_all, q_all, jnp.zeros_like(q_all))
    q_two = jnp.where(lane_all, jnp.zeros_like(q_all), q_all)
    for t in range(nt):
        qq_ref[t, :tq, :] = q_one[t * tq:(t + 1) * tq]
        qq_ref[t, tq:, :] = q_two[t * tq:(t + 1) * tq]

    qq0 = qq_ref[0]
    mx0 = jnp.full((2 * tq, W), -jnp.inf, F32)
    for c in range(S // ATTN_KEY_CHUNK):
        mx0 = score_chunk(qq0, c, sa_ref, mx0)
    m0 = jnp.max(mx0, axis=-1, keepdims=True)

    def body(j, carry):
        m_a, r_prev = carry
        t0 = 2 * j
        m_b, r_a = step(t0 + 1, sb_ref, sa_ref, m_a, jnp.maximum(t0 - 1, 0), r_prev)
        m_a, r_b = step(jnp.minimum(t0 + 2, nt - 1), sa_ref, sb_ref, m_b, t0, r_a)
        return m_a, r_b

    _, r_last = lax.fori_loop(0, nt // 2, body, (m0, jnp.ones((2 * tq, 2 * W), F32)))
    write_out(nt - 1, r_last)


def _rope_tables(S):
    half = ROT_DIM // 2
    inv = 1.0 / (ROPE_THETA ** (jnp.arange(0, ROT_DIM, 2, dtype=F32) / ROT_DIM))
    ang = jnp.arange(S, dtype=F32)[:, None] * inv[None, :]
    cos, sin = jnp.cos(ang), jnp.sin(ang)
    pad = jnp.zeros((S, DA_HD - ROT_DIM), F32)
    zero = jnp.zeros((S, half), F32)
    c = jnp.concatenate([cos, cos, pad + 1.0], axis=-1)
    s_up = jnp.concatenate([-sin, zero, pad], axis=-1)
    s_dn = jnp.concatenate([zero, sin, pad], axis=-1)
    return tuple(jnp.tile(t, (1, 2)) for t in (c, s_up, s_dn))


def _diff_attention(proj, lam_params, subln_g, layer, tq=256):
    B, S, _ = proj.shape
    W = 2 * DA_HD
    lam_init = 0.8 - 0.6 * math.exp(-0.3 * layer)

    def head(col):
        return pl.BlockSpec((None, S, W), lambda b, h: (b, 0, col * (CONV_W // W) + h))

    def const(shape):
        return pl.BlockSpec(shape, lambda b, h: (0, 0))

    return pl.pallas_call(
        functools.partial(_attn_kernel, tq=tq, lam_init=lam_init),
        grid=(B, DA_HEADS),
        in_specs=[head(COL_AQ), head(COL_AK), head(COL_AV), const((4, DA_HD)), const((1, W))],
        out_specs=pl.BlockSpec((None, S, W), lambda b, h: (b, 0, h)),
        out_shape=jax.ShapeDtypeStruct((B, S, DA_HEADS * W), BF16),
        scratch_shapes=[pltpu.VMEM((S, 2 * W), BF16),
                        pltpu.VMEM((S // tq, 2 * tq, W), BF16),
                        pltpu.VMEM((2 * tq, S), F32), pltpu.VMEM((2 * tq, S), F32)],
        compiler_params=_cparams(("parallel", "parallel")),
        name="diff_attention",
    )(proj, proj, proj, lam_params, subln_g)


def _split3(x):
    hi = x.astype(BF16)
    r1 = x - hi.astype(F32)
    mid = r1.astype(BF16)
    lo = (r1 - mid.astype(F32)).astype(BF16)
    return hi, mid, lo


def _boundary_rows(G, b, rev):
    C, K = G.shape
    e = b if rev else b - 1
    if 2 * b >= HG_SUB:
        G3 = G.reshape(C // (2 * b), 2 * b, K)
        return jnp.broadcast_to(G3[:, e:e + 1, :], G3.shape).reshape(C, K)
    assert b == 2
    G3 = G.reshape(C // HG_SUB, HG_SUB, K)
    lo = jnp.broadcast_to(G3[:, e:e + 1, :], G3.shape)
    hi = jnp.broadcast_to(G3[:, 4 + e:5 + e, :], G3.shape)
    row = lax.broadcasted_iota(jnp.int32, G3.shape, 1)
    return jnp.where(row < 4, lo, hi).reshape(C, K)


def _hgrn_masks(rev):
    C = HG_CHUNK
    r_i = lax.broadcasted_iota(jnp.int32, (C, C), 0)
    c_i = lax.broadcasted_iota(jnp.int32, (C, C), 1)
    row_k = lax.broadcasted_iota(jnp.int32, (C, HG_DK), 0)
    tri = jnp.where((c_i >= r_i) if rev else (c_i <= r_i), 1.0, 0.0).astype(BF16)
    levels = []
    b = C // 2
    while b >= 1:
        t_later = (r_i // b) % 2 == (0 if rev else 1)
        sibling = (c_i // b) == (r_i // b) + (1 if rev else -1)
        q_rows = (row_k // b) % 2 == (0 if rev else 1)
        levels.append((b, t_later & sibling, jnp.where(q_rows, 1.0, -1.0)))
        b //= 2
    return tri, levels


def _hgrn_prep(q_raw, f_raw, v_raw, lb, tri):
    forget = lb + (1.0 - lb) * _sigmoid(f_raw)
    decay = jnp.maximum(forget, GATE_FLOOR)
    hi, mid, lo = _split3(jnp.log(decay))
    G2 = (_dot(tri, hi) + _dot(tri, mid) + _dot(tri, lo)) * LOG2E
    return dict(G2=G2, q=_silu(q_raw.astype(F32)), k=1.0 - forget, v=v_raw, decay=decay)


def _hgrn_scores(t, levels, rev):
    C = HG_CHUNK
    out = []
    for h in range(HG_HEADS):
        sl = slice(h * HG_DK, (h + 1) * HG_DK)
        G2, q, k, decay = t["G2"][:, sl], t["q"][:, sl], t["k"][:, sl], t["decay"][:, sl]
        qb, kb = q.astype(BF16), k.astype(BF16)
        a = jnp.zeros((C, C), F32)
        for b, pair_mask, sign in levels:
            if b == 1:
                qt, kt = (q * decay).astype(BF16), kb
            else:
                d = G2 - _boundary_rows(G2, b, rev)
                e = jnp.exp2(d * sign).astype(BF16)
                qt, kt = qb * e, kb * e
            a = a + jnp.where(pair_mask, _dot_nt(qt, kt), 0.0)
        out.append(a.astype(BF16))
    return out


def _hgrn_finish(t, scores, states, rev):
    C = HG_CHUNK
    outs, new_states = [], []
    for h in range(HG_HEADS):
        sl = slice(h * HG_DK, (h + 1) * HG_DK)
        G2, q, k, v = t["G2"][:, sl], t["q"][:, sl], t["k"][:, sl], t["v"][:, sl]
        G2_end = G2[0:1] if rev else G2[C - 1:C]
        st = states[h]
        inter = _dot_nt((q * jnp.exp2(G2)).astype(BF16), st.astype(BF16))
        k_dec = k * jnp.exp2(G2_end - G2)
        new_states.append(st * jnp.exp2(G2_end) + _dot_tn(v, k_dec.astype(BF16)))
        intra = _dot(scores[h], v)
        diag = jnp.sum(q * k, axis=-1, keepdims=True) * v.astype(F32)
        outs.append(inter + intra + diag)
    return jnp.concatenate(outs, axis=-1), new_states


def _hgrn_kernel(qf_ref, ff_ref, vf_ref, qb_ref, fb_ref, vb_ref, lb_ref, of_ref, ob_ref, stf_ref, stb_ref):
    @pl.when(pl.program_id(1) == 0)
    def _():
        stf_ref[...] = jnp.zeros_like(stf_ref)
        stb_ref[...] = jnp.zeros_like(stb_ref)

    C = HG_CHUNK
    dirs = [dict(rev=False, q=qf_ref, f=ff_ref, v=vf_ref, lb=lb_ref[0:1, :], o=of_ref, st=stf_ref,
                 order=range(HG_STEP_CHUNKS)),
            dict(rev=True, q=qb_ref, f=fb_ref, v=vb_ref, lb=lb_ref[1:2, :], o=ob_ref, st=stb_ref,
                 order=range(HG_STEP_CHUNKS - 1, -1, -1))]
    for d in dirs:
        d["tri"], d["levels"] = _hgrn_masks(d["rev"])
        d["rows"] = [slice(j * C, (j + 1) * C) for j in d["order"]]
        d["chunks"] = [_hgrn_prep(d["q"][r, :], d["f"][r, :], d["v"][r, :], d["lb"], d["tri"])
                       for r in d["rows"]]
    for d in dirs:
        d["scores"] = [_hgrn_scores(t, d["levels"], d["rev"]) for t in d["chunks"]]
    for d in dirs:
        d["states"] = [d["st"][h] for h in range(HG_HEADS)]
    for j in range(HG_STEP_CHUNKS):
        for d in dirs:
            d["o"][d["rows"][j], :], d["states"] = _hgrn_finish(d["chunks"][j], d["scores"][j], d["states"],
                                                                d["rev"])
    for d in dirs:
        for h in range(HG_HEADS):
            d["st"][h] = d["states"][h]


def _hgrn(proj, gates, lb):
    B, S, _ = proj.shape
    C = HG_CHUNK * HG_STEP_CHUNKS
    n = S // C
    W = HG_HEADS * HG_DK

    def fwd(col):
        return pl.BlockSpec((None, C, W), lambda b, c: (b, c, col))

    def bwd(col):
        return pl.BlockSpec((None, C, W), lambda b, c: (b, n - 1 - c, col))

    return pl.pallas_call(
        _hgrn_kernel,
        grid=(B, n),
        in_specs=[fwd(COL_HQ), fwd(0), fwd(COL_HI), bwd(COL_HQ), bwd(1), bwd(COL_HI),
                  pl.BlockSpec((2, W), lambda b, c: (0, 0))],
        out_specs=[pl.BlockSpec((None, C, W), lambda b, c: (b, c, 0)),
                   pl.BlockSpec((None, C, W), lambda b, c: (b, n - 1 - c, 0))],
        out_shape=[jax.ShapeDtypeStruct((B, S, W), F32), jax.ShapeDtypeStruct((B, S, W), F32)],
        scratch_shapes=[pltpu.VMEM((HG_HEADS, HG_DK, HG_DK), F32), pltpu.VMEM((HG_HEADS, HG_DK, HG_DK), F32)],
        compiler_params=_cparams(("parallel", "arbitrary")),
        name="hgrn2",
    )(proj, gates, proj, proj, gates, proj, lb)


def _memkv_kernel(m_ref, g_ref, w_ref, o_ref):
    o_ref[...] = _dot(_rms(m_ref[...], g_ref[...]).astype(BF16), w_ref[...]).astype(o_ref.dtype)


def _mem_kv(mem, g, w):
    B, M, D = mem.shape
    N = w.shape[1]
    return pl.pallas_call(
        _memkv_kernel,
        grid=(B,),
        in_specs=[pl.BlockSpec((None, M, D), lambda b: (b, 0, 0)),
                  pl.BlockSpec((1, D), lambda b: (0, 0)),
                  pl.BlockSpec((D, N), lambda b: (0, 0))],
        out_specs=pl.BlockSpec((None, M, N), lambda b: (b, 0, 0)),
        out_shape=jax.ShapeDtypeStruct((B, M, N), BF16),
        compiler_params=_cparams(("parallel",)),
        name="mem_kv",
    )(mem, g, w)


def _merge_kernel(x_ref, c_ref, a_ref, of_ref, ob_ref, hg_ref, g0_ref, g1_ref, g2_ref, kv_ref,
                  wc_ref, wa_ref, wh_ref, wo_ref, wq_ref, wco_ref, hgn_ref, ncg_ref, o_ref, *, tm, parts):
    rows = [slice(r * (tm // parts), (r + 1) * (tm // parts)) for r in range(parts)]

    def branches(r):
        o_sum = of_ref[r, :] + ob_ref[r, :]
        gate = _silu(hg_ref[r, :].astype(F32))
        heads = []
        for h in range(HG_HEADS):
            sl = slice(h * HG_DK, (h + 1) * HG_DK)
            heads.append(_rms(o_sum[:, sl], hgn_ref[...]) * gate[:, sl])
        oh = jnp.concatenate(heads, axis=-1).astype(BF16)
        return _dot(c_ref[r, :], wc_ref[...]), _dot(a_ref[r, :], wa_ref[...]), _dot(oh, wh_ref[...])

    def mix(r, ys):
        y_c, y_a, y_h = ys
        merged = (_sigmoid(g0_ref[r, :].astype(F32)) * y_c + _sigmoid(g1_ref[r, :].astype(F32)) * y_a
                  + _sigmoid(g2_ref[r, :].astype(F32)) * y_h)
        return x_ref[r, :] + _dot(merged.astype(BF16), wo_ref[...])

    def query(x1):
        hn = _rms(x1, ncg_ref[...]).astype(BF16)
        return (_dot(hn, wq_ref[...]) * (X_HD ** -0.5 * LOG2E)).astype(BF16)

    def attend(q):
        outs = []
        for h in range(X_HEADS):
            k = kv_ref[:, h * X_HD:(h + 1) * X_HD]
            v = kv_ref[:, D_MODEL + h * X_HD:D_MODEL + (h + 1) * X_HD]
            s = _dot_nt(q[:, h * X_HD:(h + 1) * X_HD], k)
            p = jnp.exp2(s - jnp.max(s, axis=-1, keepdims=True))
            p = p * (1.0 / jnp.sum(p, axis=-1, keepdims=True))
            outs.append(_dot(p.astype(BF16), v).astype(BF16))
        return jnp.concatenate(outs, axis=-1)

    ys = [branches(r) for r in rows]
    x1 = [mix(r, y) for r, y in zip(rows, ys)]
    q = [query(x) for x in x1]
    o = [attend(t) for t in q]
    for r, x, t in zip(rows, x1, o):
        o_ref[r, :] = x + _dot(t, wco_ref[...])


def _merge_cross(x, proj, c, a, o_f, o_b, kv, wc, wa, wh, wo, wq, wco, hgn, ncg, tm=512, parts=2):
    B, S, D = x.shape
    M = kv.shape[1]
    W = CONV_W

    def tok(width, col=0):
        return pl.BlockSpec((None, tm, width), lambda b, i: (b, i, col))

    def const(shape):
        return pl.BlockSpec(shape, lambda b, i: (0, 0), pipeline_mode=pl.Buffered(1))

    return pl.pallas_call(
        functools.partial(_merge_kernel, tm=tm, parts=parts),
        grid=(B, S // tm),
        in_specs=[tok(D), tok(W), tok(W), tok(W), tok(W), tok(W, COL_HG),
                  tok(D, COL_GATE_1024), tok(D, COL_GATE_1024 + 1), tok(D, COL_GATE_1024 + 2),
                  pl.BlockSpec((None, M, 2 * D), lambda b, i: (b, 0, 0)),
                  const((W, D)), const((W, D)), const((W, D)), const((D, D)), const((D, D)), const((D, D)),
                  const((1, HG_DK)), const((1, D))],
        out_specs=tok(D),
        out_shape=jax.ShapeDtypeStruct((B, S, D), F32),
        compiler_params=_cparams(("parallel", "parallel")),
        name="merge_cross",
    )(x, c, a, o_f, o_b, proj, proj, proj, proj, kv, wc, wa, wh, wo, wq, wco, hgn, ncg)


def _ffn_kernel(xp_ref, x_ref, xn_ref, g_ref, wa_ref, wb_ref, dwa_ref, dwb_ref, ba_ref, bb_ref, wd_ref,
                fg_ref, o_ref, hn_ref, acc_ref, *, tm, final):
    i = pl.program_id(1)
    n = pl.num_programs(1)
    f = pl.program_id(2)
    nf = pl.num_programs(2)
    H = FFN_HALO

    @pl.when(f == 0)
    def _():
        g = g_ref[...]
        hn_ref[0:H, :] = jnp.where(i > 0, _rms(xp_ref[...], g), 0.0).astype(BF16)
        hn_ref[H:H + tm, :] = _rms(x_ref[...], g).astype(BF16)
        hn_ref[H + tm:H + tm + H, :] = jnp.where(i < n - 1, _rms(xn_ref[...], g), 0.0).astype(BF16)
        acc_ref[...] = jnp.zeros_like(acc_ref)

    hn = hn_ref[...]
    rows = tm + 2 * H

    def conv(u, dw_ref, b_ref):
        up = pltpu.roll(u, 1, 0)[H:H + tm]
        dn = pltpu.roll(u, rows - 1, 0)[H:H + tm]
        return dw_ref[0:1, :] * up + dw_ref[1:2, :] * u[H:H + tm] + dw_ref[2:3, :] * dn + b_ref[...]

    a = conv(_dot(hn, wa_ref[...]), dwa_ref, ba_ref)
    b = conv(_dot(hn, wb_ref[...]), dwb_ref, bb_ref)
    acc_ref[...] += _dot((_silu(a) * b).astype(BF16), wd_ref[...])

    @pl.when(f == nf - 1)
    def _():
        y = x_ref[...] + acc_ref[...]
        if final:
            y = _rms(y, fg_ref[...])
        o_ref[...] = y


def _conv_ffn(x, g, w_up, dw_w, dw_b, w_down, final_g, final, tm=1024):
    B, S, D = x.shape
    H = FFN_HALO
    nh = tm // H
    last = S // H - 1
    fc = FF_CHUNK
    nf = D_FF // fc

    def const(shape):
        return pl.BlockSpec(shape, lambda b, i, f: (0, 0))

    return pl.pallas_call(
        functools.partial(_ffn_kernel, tm=tm, final=final),
        grid=(B, S // tm, nf),
        in_specs=[
            pl.BlockSpec((None, H, D), lambda b, i, f: (b, jnp.maximum(i * nh - 1, 0), 0)),
            pl.BlockSpec((None, tm, D), lambda b, i, f: (b, i, 0)),
            pl.BlockSpec((None, H, D), lambda b, i, f: (b, jnp.minimum((i + 1) * nh, last), 0)),
            const((1, D)),
            pl.BlockSpec((D, fc), lambda b, i, f: (0, f)),
            pl.BlockSpec((D, fc), lambda b, i, f: (0, nf + f)),
            pl.BlockSpec((3, fc), lambda b, i, f: (0, f)),
            pl.BlockSpec((3, fc), lambda b, i, f: (0, nf + f)),
            pl.BlockSpec((1, fc), lambda b, i, f: (0, f)),
            pl.BlockSpec((1, fc), lambda b, i, f: (0, nf + f)),
            pl.BlockSpec((fc, D), lambda b, i, f: (f, 0)),
            const((1, D)),
        ],
        out_specs=pl.BlockSpec((None, tm, D), lambda b, i, f: (b, i, 0)),
        out_shape=jax.ShapeDtypeStruct((B, S, D), F32),
        scratch_shapes=[pltpu.VMEM((tm + 2 * H, D), BF16), pltpu.VMEM((tm, D), F32)],
        compiler_params=_cparams(("parallel", "parallel", "arbitrary")),
        name="conv_ffn",
    )(x, x, x, g, w_up, w_up, dw_w, dw_w, dw_b, dw_b, w_down, final_g)


def _trunk(x, mem, p, lbs):
    S = x.shape[1]
    rope = _rope_tables(S)
    for l in range(DEPTH):
        proj, gates, c = _inproj(x, p["norm_mix_g"][l], p["w_in"][l], p["conv_dw_w"][l], p["conv_dw_b"][l],
                                 p["conv_ln_g"][l], p["conv_ln_b"][l], rope)
        a = _diff_attention(proj, p["attn_lambda"][l], p["attn_subln_g"][l], l)
        o_f, o_b = _hgrn(proj, gates, lbs[l])
        kv = _mem_kv(mem, p["norm_mem_g"][l], p["w_ckv"][l])
        x = _merge_cross(x, proj, c, a, o_f, o_b, kv, p["w_conv_out"][l], p["w_attn_out"][l],
                         p["w_hg_out"][l], p["w_o"][l], p["w_cq"][l], p["w_co"][l],
                         p["hg_norm_g"][l], p["norm_cross_g"][l])
        x = _conv_ffn(x, p["norm_ffn_g"][l], p["w_up"][l], p["ffn_dw_w"][l], p["ffn_dw_b"][l],
                      p["w_down"][l], p["final_norm_g"], final=(l == DEPTH - 1))
    return x


def _prepare(norm_mix_g, w_in, conv_dw_w, conv_dw_b, conv_ln_g, conv_ln_b, w_conv_out, attn_lambda,
             attn_subln_g, w_attn_out, hg_lb_param, hg_norm_g, w_hg_out, w_o, norm_cross_g, norm_mem_g,
             w_cq, w_ckv, w_co, norm_ffn_g, w_up, ffn_dw_w, ffn_dw_b, w_down, final_norm_g):
    row = lambda t: t.astype(F32)[:, None, :]
    p = {
        "norm_mix_g": row(norm_mix_g), "w_in": w_in.astype(BF16),
        "conv_dw_w": conv_dw_w.astype(F32), "conv_dw_b": row(conv_dw_b),
        "conv_ln_g": row(conv_ln_g), "conv_ln_b": row(conv_ln_b), "w_conv_out": w_conv_out.astype(BF16),
        "attn_lambda": attn_lambda.astype(F32), "attn_subln_g": row(attn_subln_g),
        "w_attn_out": w_attn_out.astype(BF16),
        "hg_norm_g": row(hg_norm_g), "w_hg_out": w_hg_out.astype(BF16), "w_o": w_o.astype(BF16),
        "norm_cross_g": row(norm_cross_g), "norm_mem_g": row(norm_mem_g),
        "w_cq": w_cq.astype(BF16), "w_ckv": w_ckv.astype(BF16), "w_co": w_co.astype(BF16),
        "norm_ffn_g": row(norm_ffn_g), "w_up": w_up.astype(BF16),
        "ffn_dw_w": ffn_dw_w.astype(F32), "ffn_dw_b": row(ffn_dw_b), "w_down": w_down.astype(BF16),
        "final_norm_g": final_norm_g.astype(F32)[None, :],
    }
    lp = jax.nn.softmax(hg_lb_param.astype(F32), axis=0)
    lbs = jnp.cumsum(lp, axis=0) - lp[0:1]
    return p, lbs


def kernel(x_prompt, x_sample, mem_prompt, mem_sample, norm_mix_g, w_in, conv_dw_w, conv_dw_b, conv_ln_g, conv_ln_b, w_conv_out, attn_lambda, attn_subln_g, w_attn_out, hg_lb_param, hg_norm_g, w_hg_out, w_o, norm_cross_g, norm_mem_g, w_cq, w_ckv, w_co, norm_ffn_g, w_up, ffn_dw_w, ffn_dw_b, w_down, final_norm_g):
    p, lbs = _prepare(norm_mix_g, w_in, conv_dw_w, conv_dw_b, conv_ln_g, conv_ln_b, w_conv_out, attn_lambda,
                      attn_subln_g, w_attn_out, hg_lb_param, hg_norm_g, w_hg_out, w_o, norm_cross_g,
                      norm_mem_g, w_cq, w_ckv, w_co, norm_ffn_g, w_up, ffn_dw_w, ffn_dw_b, w_down,
                      final_norm_g)
    y_prompt = _trunk(x_prompt, mem_prompt, p, lbs)
    y_sample = _trunk(x_sample, mem_sample, p, lbs)
    return (y_prompt, y_sample)
```

```python
import functools
import math

import jax
import jax.numpy as jnp
from jax import lax
from jax.experimental import pallas as pl
from jax.experimental.pallas import tpu as pltpu

F32 = jnp.float32
BF16 = jnp.bfloat16

D_MODEL = 1024
DEPTH = 4
CONV_W = 512
CONV_K = 31
CONV_HALO = 16
DA_HEADS = 4
DA_HD = 64
ROT_DIM = 16
ROPE_THETA = 500000.0
ATTN_KEY_CHUNK = 512
HG_HEADS = 4
HG_DK = 128
HG_CHUNK = 64
HG_STEP_CHUNKS = 8
HG_SUB = 8
GATE_FLOOR = 1e-30
X_HEADS = 4
X_HD = 256
D_FF = 2816
FF_CHUNK = 1408
FFN_HALO = 8
N_IN = 8192
EPS = 1e-6

GATE_CHUNK_W = 1024
GATE_CHUNK = 3
ROPE_CHUNK = 1
COL_AQ, COL_AK, COL_AV, COL_HQ, COL_HI, COL_HG = range(6)
COL_GATE_1024 = 3
LOG2E = 1.4426950408889634
ATTN_Q_SCALE = DA_HD ** -0.5 * LOG2E

VMEM_LIMIT = 56 * 1024 * 1024


def _cparams(sem):
    return pltpu.CompilerParams(dimension_semantics=sem, vmem_limit_bytes=VMEM_LIMIT)


def _rms(x, g):
    ms = jnp.mean(x * x, axis=-1, keepdims=True)
    return x * lax.rsqrt(ms + EPS) * g


def _sigmoid(x):
    return 1.0 / (1.0 + jnp.exp2(x * -LOG2E))


def _silu(x):
    return x * _sigmoid(x)


def _dot(a, b):
    return jnp.dot(a, b, preferred_element_type=F32)


def _dot_nt(a, b):
    return lax.dot_general(a, b, (((1,), (1,)), ((), ())), preferred_element_type=F32)


def _dot_tn(a, b):
    return lax.dot_general(a, b, (((0,), (0,)), ((), ())), preferred_element_type=F32)


def _rope(x, c, s_up, s_dn):
    n = x.shape[-1]
    half = ROT_DIM // 2
    return x * c + pltpu.roll(x, n - half, 1) * s_up + pltpu.roll(x, half, 1) * s_dn


def _inproj_kernel(xp_ref, x_ref, xn_ref, g_ref, w_ref, cw_ref, cb_ref, lg_ref, lb_ref, rc_ref, rsu_ref, rsd_ref,
                   o_ref, gate_ref, c_ref, xe_ref, ext_ref, sh_ref, cwb_ref, *, tm, sub):
    i = pl.program_id(1)
    n = pl.num_programs(1)
    H = CONV_HALO
    SUBL = HG_SUB
    tn = GATE_CHUNK_W
    g = g_ref[...]
    xe_ref[0:H, :] = jnp.where(i > 0, _rms(xp_ref[...], g), 0.0).astype(BF16)
    xe_ref[H:H + tm, :] = _rms(x_ref[...], g).astype(BF16)
    xe_ref[H + tm:H + tm + H, :] = jnp.where(i < n - 1, _rms(xn_ref[...], g), 0.0).astype(BF16)

    y = _dot(xe_ref[...], w_ref[:, 0:tn])
    ext_ref[...] = y[:, :CONV_W] * _sigmoid(y[:, CONV_W:])
    rows_sh = sh_ref.shape[1]
    for r in range(SUBL):
        sh_ref[r] = ext_ref[pl.ds(r, rows_sh), :]

    for j in range(CONV_K):
        cwb_ref[j] = jnp.broadcast_to(cw_ref[j:j + 1, :], (SUBL, CONV_W))

    def conv_rows(r):
        off = H - CONV_K // 2
        acc = jnp.zeros((sub // SUBL, SUBL, CONV_W), F32)
        for j in range(CONV_K):
            start = j + off
            taps = sh_ref[start % SUBL, pl.ds(r * sub + start - start % SUBL, sub), :]
            acc = acc + cwb_ref[j][None] * taps.reshape(sub // SUBL, SUBL, CONV_W)
        acc = acc.reshape(sub, CONV_W) + cb_ref[...]
        mu = jnp.mean(acc, axis=-1, keepdims=True)
        d = acc - mu
        var = jnp.mean(d * d, axis=-1, keepdims=True)
        z = d * lax.rsqrt(var + EPS) * lg_ref[...] + lb_ref[...]
        c_ref[r * sub:(r + 1) * sub, :] = _silu(z).astype(c_ref.dtype)

    xn = xe_ref[H:H + tm, :]
    n_chunks = N_IN // tn
    pieces = tm // sub
    done = 0
    for j in range(1, n_chunks):
        y = _dot(xn, w_ref[:, j * tn:(j + 1) * tn])
        jo = j - 1 - (j > GATE_CHUNK)
        if j == GATE_CHUNK:
            gate_ref[...] = y
        elif j == ROPE_CHUNK:
            W = 2 * DA_HD
            for h in range(tn // W):
                r = _rope(y[:, h * W:(h + 1) * W], rc_ref[...], rsu_ref[...], rsd_ref[...])
                if h < DA_HEADS:
                    r = r * ATTN_Q_SCALE
                o_ref[:, jo * tn + h * W:jo * tn + (h + 1) * W] = r.astype(o_ref.dtype)
        else:
            o_ref[:, jo * tn:(jo + 1) * tn] = y.astype(o_ref.dtype)
        upto = (pieces * j) // (n_chunks - 1)
        for r in range(done, upto):
            conv_rows(r)
        done = upto


def _inproj(x, g, w, dw_w, dw_b, ln_g, ln_b, rope, tm=512, sub=16):
    B, S, D = x.shape
    N = w.shape[1]
    H = CONV_HALO
    nh = tm // H
    last = S // H - 1
    n_out = N - 2 * GATE_CHUNK_W

    def const(shape):
        return pl.BlockSpec(shape, lambda b, i: (0, 0))

    def tok(width):
        return pl.BlockSpec((None, tm, width), lambda b, i: (b, i, 0))

    rope_rows = pl.BlockSpec((tm, 2 * DA_HD), lambda b, i: (i, 0))

    return pl.pallas_call(
        functools.partial(_inproj_kernel, tm=tm, sub=sub),
        grid=(B, S // tm),
        in_specs=[
            pl.BlockSpec((None, H, D), lambda b, i: (b, jnp.maximum(i * nh - 1, 0), 0)),
            tok(D),
            pl.BlockSpec((None, H, D), lambda b, i: (b, jnp.minimum((i + 1) * nh, last), 0)),
            const((1, D)),
            pl.BlockSpec((D, N), lambda b, i: (0, 0), pipeline_mode=pl.Buffered(1)),
            const((CONV_K, CONV_W)), const((1, CONV_W)), const((1, CONV_W)), const((1, CONV_W)),
            rope_rows, rope_rows, rope_rows,
        ],
        out_specs=[tok(n_out), tok(GATE_CHUNK_W), tok(CONV_W)],
        out_shape=[jax.ShapeDtypeStruct((B, S, n_out), BF16),
                   jax.ShapeDtypeStruct((B, S, GATE_CHUNK_W), F32),
                   jax.ShapeDtypeStruct((B, S, CONV_W), BF16)],
        scratch_shapes=[pltpu.VMEM((tm + 2 * H, D), BF16),
                        pltpu.VMEM((tm + 2 * H, CONV_W), F32),
                        pltpu.VMEM((HG_SUB, tm + 2 * H - HG_SUB, CONV_W), F32),
                        pltpu.VMEM((CONV_K, HG_SUB, CONV_W), F32)],
        compiler_params=_cparams(("parallel", "parallel")),
        name="inproj_conv",
    )(x, x, x, g, w, dw_w, dw_b, ln_g, ln_b, *rope)


def _attn_kernel(q_ref, k_ref, v_ref, lam_ref, g_ref, o_ref, vx_ref, qq_ref, sa_ref, sb_ref, *, tq, lam_init):
    S = k_ref.shape[0]
    W = 2 * DA_HD
    nt = S // tq
    vx_ref[:, :W] = v_ref[...]
    vx_ref[:, W:] = jnp.where(lax.broadcasted_iota(jnp.int32, (S, W), 1) == 0, 1.0, 0.0).astype(BF16)

    lp = lam_ref[...]
    lam = (jnp.exp(jnp.sum(lp[0:1] * lp[1:2], axis=-1, keepdims=True))
           - jnp.exp(jnp.sum(lp[2:3] * lp[3:4], axis=-1, keepdims=True)) + lam_init)

    def rows_of(t):
        return pl.ds(pl.multiple_of(t * tq, tq), tq)

    def score_chunk(qq, c, s_ref, mx):
        keys = slice(c * ATTN_KEY_CHUNK, (c + 1) * ATTN_KEY_CHUNK)
        s = _dot_nt(qq, k_ref[keys, :])
        s_ref[:, keys] = s
        for i in range(ATTN_KEY_CHUNK // W):
            mx = jnp.maximum(mx, s[:, i * W:(i + 1) * W])
        return mx

    def write_out(t, r):
        inv = 1.0 / r[:, W:W + 1]
        o = r[:tq, :W] * inv[:tq] - (lam * inv[tq:]) * r[tq:, :W]
        o_ref[rows_of(t), :] = (_rms(o, g_ref[...]) * (1.0 - lam_init)).astype(o_ref.dtype)

    def step(t_next, nxt_ref, cur_ref, m_cur, t_prev, r_prev):
        qq = qq_ref[t_next]
        mx = jnp.full((2 * tq, W), -jnp.inf, F32)
        r = jnp.zeros((2 * tq, 2 * W), F32)
        for c in range(S // ATTN_KEY_CHUNK):
            keys = slice(c * ATTN_KEY_CHUNK, (c + 1) * ATTN_KEY_CHUNK)
            mx = score_chunk(qq, c, nxt_ref, mx)
            p = jnp.exp2(cur_ref[:, keys] - m_cur).astype(BF16)
            r = r + _dot(p, vx_ref[keys, :])
            if c == 0:
                write_out(t_prev, r_prev)
        return jnp.max(mx, axis=-1, keepdims=True), r

    q_all = q_ref[...]
    lane_all = lax.broadcasted_iota(jnp.int32, (S, W), 1) < DA_HD
    q_one = jnp.where(lane_all, q_all, jnp.zeros_like(q_all))
    q_two = jnp.where(lane_all, jnp.zeros_like(q_all), q_all)
    for t in range(nt):
        qq_ref[t, :tq, :] = q_one[t * tq:(t + 1) * tq]
        qq_ref[t, tq:, :] = q_two[t * tq:(t + 1) * tq]

    qq0 = qq_ref[0]
    mx0 = jnp.full((2 * tq, W), -jnp.inf, F32)
    for c in range(S // ATTN_KEY_CHUNK):
        mx0 = score_chunk(qq0, c, sa_ref, mx0)
    m0 = jnp.max(mx0, axis=-1, keepdims=True)

    def body(j, carry):
        m_a, r_prev = carry
        t0 = 2 * j
        m_b, r_a = step(t0 + 1, sb_ref, sa_ref, m_a, jnp.maximum(t0 - 1, 0), r_prev)
        m_a, r_b = step(jnp.minimum(t0 + 2, nt - 1), sa_ref, sb_ref, m_b, t0, r_a)
        return m_a, r_b

    _, r_last = lax.fori_loop(0, nt // 2, body, (m0, jnp.ones((2 * tq, 2 * W), F32)))
    write_out(nt - 1, r_last)


def _rope_tables(S):
    half = ROT_DIM // 2
    inv = 1.0 / (ROPE_THETA ** (jnp.arange(0, ROT_DIM, 2, dtype=F32) / ROT_DIM))
    ang = jnp.arange(S, dtype=F32)[:, None] * inv[None, :]
    cos, sin = jnp.cos(ang), jnp.sin(ang)
    pad = jnp.zeros((S, DA_HD - ROT_DIM), F32)
    zero = jnp.zeros((S, half), F32)
    c = jnp.concatenate([cos, cos, pad + 1.0], axis=-1)
    s_up = jnp.concatenate([-sin, zero, pad], axis=-1)
    s_dn = jnp.concatenate([zero, sin, pad], axis=-1)
    return tuple(jnp.tile(t, (1, 2)) for t in (c, s_up, s_dn))


def _diff_attention(proj, lam_params, subln_g, layer, tq=256):
    B, S, _ = proj.shape
    W = 2 * DA_HD
    lam_init = 0.8 - 0.6 * math.exp(-0.3 * layer)

    def head(col):
        return pl.BlockSpec((None, S, W), lambda b, h: (b, 0, col * (CONV_W // W) + h))

    def const(shape):
        return pl.BlockSpec(shape, lambda b, h: (0, 0))

    return pl.pallas_call(
        functools.partial(_attn_kernel, tq=tq, lam_init=lam_init),
        grid=(B, DA_HEADS),
        in_specs=[head(COL_AQ), head(COL_AK), head(COL_AV), const((4, DA_HD)), const((1, W))],
        out_specs=pl.BlockSpec((None, S, W), lambda b, h: (b, 0, h)),
        out_shape=jax.ShapeDtypeStruct((B, S, DA_HEADS * W), BF16),
        scratch_shapes=[pltpu.VMEM((S, 2 * W), BF16),
                        pltpu.VMEM((S // tq, 2 * tq, W), BF16),
                        pltpu.VMEM((2 * tq, S), F32), pltpu.VMEM((2 * tq, S), F32)],
        compiler_params=_cparams(("parallel", "parallel")),
        name="diff_attention",
    )(proj, proj, proj, lam_params, subln_g)


def _split3(x):
    hi = x.astype(BF16)
    r1 = x - hi.astype(F32)
    mid = r1.astype(BF16)
    lo = (r1 - mid.astype(F32)).astype(BF16)
    return hi, mid, lo


def _boundary_rows(G, b, rev):
    C, K = G.shape
    e = b if rev else b - 1
    if 2 * b >= HG_SUB:
        G3 = G.reshape(C // (2 * b), 2 * b, K)
        return jnp.broadcast_to(G3[:, e:e + 1, :], G3.shape).reshape(C, K)
    assert b == 2
    G3 = G.reshape(C // HG_SUB, HG_SUB, K)
    lo = jnp.broadcast_to(G3[:, e:e + 1, :], G3.shape)
    hi = jnp.broadcast_to(G3[:, 4 + e:5 + e, :], G3.shape)
    row = lax.broadcasted_iota(jnp.int32, G3.shape, 1)
    return jnp.where(row < 4, lo, hi).reshape(C, K)


def _hgrn_masks(rev):
    C = HG_CHUNK
    r_i = lax.broadcasted_iota(jnp.int32, (C, C), 0)
    c_i = lax.broadcasted_iota(jnp.int32, (C, C), 1)
    row_k = lax.broadcasted_iota(jnp.int32, (C, HG_DK), 0)
    tri = jnp.where((c_i >= r_i) if rev else (c_i <= r_i), 1.0, 0.0).astype(BF16)
    levels = []
    b = C // 2
    while b >= 1:
        t_later = (r_i // b) % 2 == (0 if rev else 1)
        sibling = (c_i // b) == (r_i // b) + (1 if rev else -1)
        q_rows = (row_k // b) % 2 == (0 if rev else 1)
        levels.append((b, t_later & sibling, jnp.where(q_rows, 1.0, -1.0)))
        b //= 2
    return tri, levels


def _hgrn_prep(q_raw, f_raw, v_raw, lb, tri):
    forget = lb + (1.0 - lb) * _sigmoid(f_raw)
    decay = jnp.maximum(forget, GATE_FLOOR)
    hi, mid, lo = _split3(jnp.log(decay))
    G2 = (_dot(tri, hi) + _dot(tri, mid) + _dot(tri, lo)) * LOG2E
    return dict(G2=G2, q=_silu(q_raw.astype(F32)), k=1.0 - forget, v=v_raw, decay=decay)


def _hgrn_scores(t, levels, rev):
    C = HG_CHUNK
    out = []
    for h in range(HG_HEADS):
        sl = slice(h * HG_DK, (h + 1) * HG_DK)
        G2, q, k, decay = t["G2"][:, sl], t["q"][:, sl], t["k"][:, sl], t["decay"][:, sl]
        qb, kb = q.astype(BF16), k.astype(BF16)
        a = jnp.zeros((C, C), F32)
        for b, pair_mask, sign in levels:
            if b == 1:
                qt, kt = (q * decay).astype(BF16), kb
            else:
                d = G2 - _boundary_rows(G2, b, rev)
                e = jnp.exp2(d * sign).astype(BF16)
                qt, kt = qb * e, kb * e
            a = a + jnp.where(pair_mask, _dot_nt(qt, kt), 0.0)
        out.append(a.astype(BF16))
    return out


def _hgrn_finish(t, scores, states, rev):
    C = HG_CHUNK
    outs, new_states = [], []
    for h in range(HG_HEADS):
        sl = slice(h * HG_DK, (h + 1) * HG_DK)
        G2, q, k, v = t["G2"][:, sl], t["q"][:, sl], t["k"][:, sl], t["v"][:, sl]
        G2_end = G2[0:1] if rev else G2[C - 1:C]
        st = states[h]
        inter = _dot_nt((q * jnp.exp2(G2)).astype(BF16), st.astype(BF16))
        k_dec = k * jnp.exp2(G2_end - G2)
        new_states.append(st * jnp.exp2(G2_end) + _dot_tn(v, k_dec.astype(BF16)))
        intra = _dot(scores[h], v)
        diag = jnp.sum(q * k, axis=-1, keepdims=True) * v.astype(F32)
        outs.append(inter + intra + diag)
    return jnp.concatenate(outs, axis=-1), new_states


def _hgrn_kernel(qf_ref, ff_ref, vf_ref, qb_ref, fb_ref, vb_ref, lb_ref, of_ref, ob_ref, stf_ref, stb_ref):
    @pl.when(pl.program_id(1) == 0)
    def _():
        stf_ref[...] = jnp.zeros_like(stf_ref)
        stb_ref[...] = jnp.zeros_like(stb_ref)

    C = HG_CHUNK
    dirs = [dict(rev=False, q=qf_ref, f=ff_ref, v=vf_ref, lb=lb_ref[0:1, :], o=of_ref, st=stf_ref,
                 order=range(HG_STEP_CHUNKS)),
            dict(rev=True, q=qb_ref, f=fb_ref, v=vb_ref, lb=lb_ref[1:2, :], o=ob_ref, st=stb_ref,
                 order=range(HG_STEP_CHUNKS - 1, -1, -1))]
    for d in dirs:
        d["tri"], d["levels"] = _hgrn_masks(d["rev"])
        d["rows"] = [slice(j * C, (j + 1) * C) for j in d["order"]]
        d["chunks"] = [_hgrn_prep(d["q"][r, :], d["f"][r, :], d["v"][r, :], d["lb"], d["tri"])
                       for r in d["rows"]]
    for d in dirs:
        d["scores"] = [_hgrn_scores(t, d["levels"], d["rev"]) for t in d["chunks"]]
    for d in dirs:
        d["states"] = [d["st"][h] for h in range(HG_HEADS)]
    for j in range(HG_STEP_CHUNKS):
        for d in dirs:
            d["o"][d["rows"][j], :], d["states"] = _hgrn_finish(d["chunks"][j], d["scores"][j], d["states"],
                                                                d["rev"])
    for d in dirs:
        for h in range(HG_HEADS):
            d["st"][h] = d["states"][h]


def _hgrn(proj, gates, lb):
    B, S, _ = proj.shape
    C = HG_CHUNK * HG_STEP_CHUNKS
    n = S // C
    W = HG_HEADS * HG_DK

    def fwd(col):
        return pl.BlockSpec((None, C, W), lambda b, c: (b, c, col))

    def bwd(col):
        return pl.BlockSpec((None, C, W), lambda b, c: (b, n - 1 - c, col))

    return pl.pallas_call(
        _hgrn_kernel,
        grid=(B, n),
        in_specs=[fwd(COL_HQ), fwd(0), fwd(COL_HI), bwd(COL_HQ), bwd(1), bwd(COL_HI),
                  pl.BlockSpec((2, W), lambda b, c: (0, 0))],
        out_specs=[pl.BlockSpec((None, C, W), lambda b, c: (b, c, 0)),
                   pl.BlockSpec((None, C, W), lambda b, c: (b, n - 1 - c, 0))],
        out_shape=[jax.ShapeDtypeStruct((B, S, W), F32), jax.ShapeDtypeStruct((B, S, W), F32)],
        scratch_shapes=[pltpu.VMEM((HG_HEADS, HG_DK, HG_DK), F32), pltpu.VMEM((HG_HEADS, HG_DK, HG_DK), F32)],
        compiler_params=_cparams(("parallel", "arbitrary")),
        name="hgrn2",
    )(proj, gates, proj, proj, gates, proj, lb)


def _memkv_kernel(m_ref, g_ref, w_ref, o_ref):
    o_ref[...] = _dot(_rms(m_ref[...], g_ref[...]).astype(BF16), w_ref[...]).astype(o_ref.dtype)


def _mem_kv(mem, g, w):
    B, M, D = mem.shape
    N = w.shape[1]
    return pl.pallas_call(
        _memkv_kernel,
        grid=(B,),
        in_specs=[pl.BlockSpec((None, M, D), lambda b: (b, 0, 0)),
                  pl.BlockSpec((1, D), lambda b: (0, 0)),
                  pl.BlockSpec((D, N), lambda b: (0, 0))],
        out_specs=pl.BlockSpec((None, M, N), lambda b: (b, 0, 0)),
        out_shape=jax.ShapeDtypeStruct((B, M, N), BF16),
        compiler_params=_cparams(("parallel",)),
        name="mem_kv",
    )(mem, g, w)


def _merge_kernel(x_ref, c_ref, a_ref, of_ref, ob_ref, hg_ref, g0_ref, g1_ref, g2_ref, kv_ref,
                  wc_ref, wa_ref, wh_ref, wo_ref, wq_ref, wco_ref, hgn_ref, ncg_ref, o_ref, *, tm, parts):
    rows = [slice(r * (tm // parts), (r + 1) * (tm // parts)) for r in range(parts)]

    def hgrn_out(r):
        o_sum = of_ref[r, :] + ob_ref[r, :]
        gate = _silu(hg_ref[r, :].astype(F32))
        heads = []
        for h in range(HG_HEADS):
            sl = slice(h * HG_DK, (h + 1) * HG_DK)
            heads.append(_rms(o_sum[:, sl], hgn_ref[...]) * gate[:, sl])
        return jnp.concatenate(heads, axis=-1).astype(BF16)

    def gates(r):
        return [_sigmoid(g_ref[r, :].astype(F32)) for g_ref in (g0_ref, g1_ref, g2_ref)]

    def branches(r, oh):
        return _dot(c_ref[r, :], wc_ref[...]), _dot(a_ref[r, :], wa_ref[...]), _dot(oh, wh_ref[...])

    def mix(r, ys, gs):
        merged = gs[0] * ys[0] + gs[1] * ys[1] + gs[2] * ys[2]
        return x_ref[r, :] + _dot(merged.astype(BF16), wo_ref[...])

    def query(x1):
        hn = _rms(x1, ncg_ref[...]).astype(BF16)
        return (_dot(hn, wq_ref[...]) * (X_HD ** -0.5 * LOG2E)).astype(BF16)

    def scores(q, h):
        return _dot_nt(q[:, h * X_HD:(h + 1) * X_HD], kv_ref[:, h * X_HD:(h + 1) * X_HD])

    def softmax(s):
        p = jnp.exp2(s - jnp.max(s, axis=-1, keepdims=True))
        return (p * (1.0 / jnp.sum(p, axis=-1, keepdims=True))).astype(BF16)

    def values(p, h):
        return _dot(p, kv_ref[:, D_MODEL + h * X_HD:D_MODEL + (h + 1) * X_HD]).astype(BF16)

    oh = [hgrn_out(r) for r in rows]
    ys, gs = [], []
    for r, t in zip(rows, oh):
        ys.append(branches(r, t))
        gs.append(gates(r))
    x1 = [mix(r, y, g) for r, y, g in zip(rows, ys, gs)]
    q = [query(x) for x in x1]
    s = [[scores(t, h) for h in range(X_HEADS)] for t in q]
    p = [[softmax(t) for t in per_group] for per_group in s]
    o = [jnp.concatenate([values(t, h) for h, t in enumerate(per_group)], axis=-1) for per_group in p]
    for r, x, t in zip(rows, x1, o):
        o_ref[r, :] = x + _dot(t, wco_ref[...])


def _merge_cross(x, proj, c, a, o_f, o_b, kv, wc, wa, wh, wo, wq, wco, hgn, ncg, tm=512, parts=2):
    B, S, D = x.shape
    M = kv.shape[1]
    W = CONV_W

    def tok(width, col=0):
        return pl.BlockSpec((None, tm, width), lambda b, i: (b, i, col))

    def const(shape):
        return pl.BlockSpec(shape, lambda b, i: (0, 0), pipeline_mode=pl.Buffered(1))

    return pl.pallas_call(
        functools.partial(_merge_kernel, tm=tm, parts=parts),
        grid=(B, S // tm),
        in_specs=[tok(D), tok(W), tok(W), tok(W), tok(W), tok(W, COL_HG),
                  tok(D, COL_GATE_1024), tok(D, COL_GATE_1024 + 1), tok(D, COL_GATE_1024 + 2),
                  pl.BlockSpec((None, M, 2 * D), lambda b, i: (b, 0, 0)),
                  const((W, D)), const((W, D)), const((W, D)), const((D, D)), const((D, D)), const((D, D)),
                  const((1, HG_DK)), const((1, D))],
        out_specs=tok(D),
        out_shape=jax.ShapeDtypeStruct((B, S, D), F32),
        compiler_params=_cparams(("parallel", "parallel")),
        name="merge_cross",
    )(x, c, a, o_f, o_b, proj, proj, proj, proj, kv, wc, wa, wh, wo, wq, wco, hgn, ncg)


def _ffn_kernel(xp_ref, x_ref, xn_ref, g_ref, wa_ref, wb_ref, dwa_ref, dwb_ref, ba_ref, bb_ref, wd_ref,
                fg_ref, o_ref, hn_ref, acc_ref, *, tm, final):
    i = pl.program_id(1)
    n = pl.num_programs(1)
    f = pl.program_id(2)
    nf = pl.num_programs(2)
    H = FFN_HALO

    @pl.when(f == 0)
    def _():
        g = g_ref[...]
        hn_ref[0:H, :] = jnp.where(i > 0, _rms(xp_ref[...], g), 0.0).astype(BF16)
        hn_ref[H:H + tm, :] = _rms(x_ref[...], g).astype(BF16)
        hn_ref[H + tm:H + tm + H, :] = jnp.where(i < n - 1, _rms(xn_ref[...], g), 0.0).astype(BF16)
        acc_ref[...] = jnp.zeros_like(acc_ref)

    hn = hn_ref[...]
    rows = tm + 2 * H

    def conv(u, dw_ref, b_ref):
        up = pltpu.roll(u, 1, 0)[H:H + tm]
        dn = pltpu.roll(u, rows - 1, 0)[H:H + tm]
        return dw_ref[0:1, :] * up + dw_ref[1:2, :] * u[H:H + tm] + dw_ref[2:3, :] * dn + b_ref[...]

    a = conv(_dot(hn, wa_ref[...]), dwa_ref, ba_ref)
    b = conv(_dot(hn, wb_ref[...]), dwb_ref, bb_ref)
    acc_ref[...] += _dot((_silu(a) * b).astype(BF16), wd_ref[...])

    @pl.when(f == nf - 1)
    def _():
        y = x_ref[...] + acc_ref[...]
        if final:
            y = _rms(y, fg_ref[...])
        o_ref[...] = y


def _conv_ffn(x, g, w_up, dw_w, dw_b, w_down, final_g, final, tm=1024):
    B, S, D = x.shape
    H = FFN_HALO
    nh = tm // H
    last = S // H - 1
    fc = FF_CHUNK
    nf = D_FF // fc

    def const(shape):
        return pl.BlockSpec(shape, lambda b, i, f: (0, 0))

    return pl.pallas_call(
        functools.partial(_ffn_kernel, tm=tm, final=final),
        grid=(B, S // tm, nf),
        in_specs=[
            pl.BlockSpec((None, H, D), lambda b, i, f: (b, jnp.maximum(i * nh - 1, 0), 0)),
            pl.BlockSpec((None, tm, D), lambda b, i, f: (b, i, 0)),
            pl.BlockSpec((None, H, D), lambda b, i, f: (b, jnp.minimum((i + 1) * nh, last), 0)),
            const((1, D)),
            pl.BlockSpec((D, fc), lambda b, i, f: (0, f)),
            pl.BlockSpec((D, fc), lambda b, i, f: (0, nf + f)),
            pl.BlockSpec((3, fc), lambda b, i, f: (0, f)),
            pl.BlockSpec((3, fc), lambda b, i, f: (0, nf + f)),
            pl.BlockSpec((1, fc), lambda b, i, f: (0, f)),
            pl.BlockSpec((1, fc), lambda b, i, f: (0, nf + f)),
            pl.BlockSpec((fc, D), lambda b, i, f: (f, 0)),
            const((1, D)),
        ],
        out_specs=pl.BlockSpec((None, tm, D), lambda b, i, f: (b, i, 0)),
        out_shape=jax.ShapeDtypeStruct((B, S, D), F32),
        scratch_shapes=[pltpu.VMEM((tm + 2 * H, D), BF16), pltpu.VMEM((tm, D), F32)],
        compiler_params=_cparams(("parallel", "parallel", "arbitrary")),
        name="conv_ffn",
    )(x, x, x, g, w_up, w_up, dw_w, dw_w, dw_b, dw_b, w_down, final_g)


def _trunk(x, mem, p, lbs):
    S = x.shape[1]
    rope = _rope_tables(S)
    for l in range(DEPTH):
        proj, gates, c = _inproj(x, p["norm_mix_g"][l], p["w_in"][l], p["conv_dw_w"][l], p["conv_dw_b"][l],
                                 p["conv_ln_g"][l], p["conv_ln_b"][l], rope)
        a = _diff_attention(proj, p["attn_lambda"][l], p["attn_subln_g"][l], l)
        o_f, o_b = _hgrn(proj, gates, lbs[l])
        kv = _mem_kv(mem, p["norm_mem_g"][l], p["w_ckv"][l])
        x = _merge_cross(x, proj, c, a, o_f, o_b, kv, p["w_conv_out"][l], p["w_attn_out"][l],
                         p["w_hg_out"][l], p["w_o"][l], p["w_cq"][l], p["w_co"][l],
                         p["hg_norm_g"][l], p["norm_cross_g"][l])
        x = _conv_ffn(x, p["norm_ffn_g"][l], p["w_up"][l], p["ffn_dw_w"][l], p["ffn_dw_b"][l],
                      p["w_down"][l], p["final_norm_g"], final=(l == DEPTH - 1))
    return x


def _prepare(norm_mix_g, w_in, conv_dw_w, conv_dw_b, conv_ln_g, conv_ln_b, w_conv_out, attn_lambda,
             attn_subln_g, w_attn_out, hg_lb_param, hg_norm_g, w_hg_out, w_o, norm_cross_g, norm_mem_g,
             w_cq, w_ckv, w_co, norm_ffn_g, w_up, ffn_dw_w, ffn_dw_b, w_down, final_norm_g):
    row = lambda t: t.astype(F32)[:, None, :]
    p = {
        "norm_mix_g": row(norm_mix_g), "w_in": w_in.astype(BF16),
        "conv_dw_w": conv_dw_w.astype(F32), "conv_dw_b": row(conv_dw_b),
        "conv_ln_g": row(conv_ln_g), "conv_ln_b": row(conv_ln_b), "w_conv_out": w_conv_out.astype(BF16),
        "attn_lambda": attn_lambda.astype(F32), "attn_subln_g": row(attn_subln_g),
        "w_attn_out": w_attn_out.astype(BF16),
        "hg_norm_g": row(hg_norm_g), "w_hg_out": w_hg_out.astype(BF16), "w_o": w_o.astype(BF16),
        "norm_cross_g": row(norm_cross_g), "norm_mem_g": row(norm_mem_g),
        "w_cq": w_cq.astype(BF16), "w_ckv": w_ckv.astype(BF16), "w_co": w_co.astype(BF16),
        "norm_ffn_g": row(norm_ffn_g), "w_up": w_up.astype(BF16),
        "ffn_dw_w": ffn_dw_w.astype(F32), "ffn_dw_b": row(ffn_dw_b), "w_down": w_down.astype(BF16),
        "final_norm_g": final_norm_g.astype(F32)[None, :],
    }
    lp = jax.nn.softmax(hg_lb_param.astype(F32), axis=0)
    lbs = jnp.cumsum(lp, axis=0) - lp[0:1]
    return p, lbs


def kernel(x_prompt, x_sample, mem_prompt, mem_sample, norm_mix_g, w_in, conv_dw_w, conv_dw_b, conv_ln_g, conv_ln_b, w_conv_out, attn_lambda, attn_subln_g, w_attn_out, hg_lb_param, hg_norm_g, w_hg_out, w_o, norm_cross_g, norm_mem_g, w_cq, w_ckv, w_co, norm_ffn_g, w_up, ffn_dw_w, ffn_dw_b, w_down, final_norm_g):
    p, lbs = _prepare(norm_mix_g, w_in, conv_dw_w, conv_dw_b, conv_ln_g, conv_ln_b, w_conv_out, attn_lambda,
                      attn_subln_g, w_attn_out, hg_lb_param, hg_norm_g, w_hg_out, w_o, norm_cross_g,
                      norm_mem_g, w_cq, w_ckv, w_co, norm_ffn_g, w_up, ffn_dw_w, ffn_dw_b, w_down,
                      final_norm_g)
    y_prompt = _trunk(x_prompt, mem_prompt, p, lbs)
    y_sample = _trunk(x_sample, mem_sample, p, lbs)
    return (y_prompt, y_sample)
```

```python
import functools
import math

import jax
import jax.numpy as jnp
from jax import lax
from jax.experimental import pallas as pl
from jax.experimental.pallas import tpu as pltpu

F32 = jnp.float32
BF16 = jnp.bfloat16

D_MODEL = 1024
DEPTH = 4
CONV_W = 512
CONV_K = 31
CONV_HALO = 16
DA_HEADS = 4
DA_HD = 64
ROT_DIM = 16
ROPE_THETA = 500000.0
ATTN_KEY_CHUNK = 512
ATTN_TILES_PER_TRIP = 8
HG_HEADS = 4
HG_DK = 128
HG_CHUNK = 64
HG_STEP_CHUNKS = 8
HG_SUB = 8
GATE_FLOOR = 1e-30
X_HEADS = 4
X_HD = 256
D_FF = 2816
FF_CHUNK = 1408
FFN_HALO = 8
N_IN = 8192
EPS = 1e-6

GATE_CHUNK_W = 1024
GATE_CHUNK = 3
ROPE_CHUNK = 1
COL_AQ, COL_AK, COL_AV, COL_HQ, COL_HI, COL_HG = range(6)
COL_GATE_1024 = 3
LOG2E = 1.4426950408889634
ATTN_Q_SCALE = DA_HD ** -0.5 * LOG2E

VMEM_LIMIT = 56 * 1024 * 1024


def _cparams(sem):
    return pltpu.CompilerParams(dimension_semantics=sem, vmem_limit_bytes=VMEM_LIMIT)


def _rms(x, g):
    ms = jnp.mean(x * x, axis=-1, keepdims=True)
    return x * lax.rsqrt(ms + EPS) * g


def _sigmoid(x):
    return 1.0 / (1.0 + jnp.exp2(x * -LOG2E))


def _silu(x):
    return x * _sigmoid(x)


def _dot(a, b):
    return jnp.dot(a, b, preferred_element_type=F32)


def _dot_nt(a, b):
    return lax.dot_general(a, b, (((1,), (1,)), ((), ())), preferred_element_type=F32)


def _dot_tn(a, b):
    return lax.dot_general(a, b, (((0,), (0,)), ((), ())), preferred_element_type=F32)


def _rope(x, c, s_up, s_dn):
    n = x.shape[-1]
    half = ROT_DIM // 2
    return x * c + pltpu.roll(x, n - half, 1) * s_up + pltpu.roll(x, half, 1) * s_dn


def _inproj_kernel(xp_ref, x_ref, xn_ref, g_ref, w_ref, cw_ref, cb_ref, lg_ref, lb_ref, rc_ref, rsu_ref, rsd_ref,
                   o_ref, gate_ref, c_ref, xe_ref, ext_ref, sh_ref, cwb_ref, *, tm, sub):
    i = pl.program_id(1)
    n = pl.num_programs(1)
    H = CONV_HALO
    SUBL = HG_SUB
    tn = GATE_CHUNK_W
    g = g_ref[...]
    xe_ref[0:H, :] = jnp.where(i > 0, _rms(xp_ref[...], g), 0.0).astype(BF16)
    xe_ref[H:H + tm, :] = _rms(x_ref[...], g).astype(BF16)
    xe_ref[H + tm:H + tm + H, :] = jnp.where(i < n - 1, _rms(xn_ref[...], g), 0.0).astype(BF16)

    y = _dot(xe_ref[...], w_ref[:, 0:tn])
    ext_ref[...] = y[:, :CONV_W] * _sigmoid(y[:, CONV_W:])
    rows_sh = sh_ref.shape[1]
    for r in range(SUBL):
        sh_ref[r] = ext_ref[pl.ds(r, rows_sh), :]

    for j in range(CONV_K):
        cwb_ref[j] = jnp.broadcast_to(cw_ref[j:j + 1, :], (SUBL, CONV_W))

    def conv_rows(r):
        off = H - CONV_K // 2
        acc = jnp.zeros((sub // SUBL, SUBL, CONV_W), F32)
        for j in range(CONV_K):
            start = j + off
            taps = sh_ref[start % SUBL, pl.ds(r * sub + start - start % SUBL, sub), :]
            acc = acc + cwb_ref[j][None] * taps.reshape(sub // SUBL, SUBL, CONV_W)
        acc = acc.reshape(sub, CONV_W) + cb_ref[...]
        mu = jnp.mean(acc, axis=-1, keepdims=True)
        d = acc - mu
        var = jnp.mean(d * d, axis=-1, keepdims=True)
        z = d * lax.rsqrt(var + EPS) * lg_ref[...] + lb_ref[...]
        c_ref[r * sub:(r + 1) * sub, :] = _silu(z).astype(c_ref.dtype)

    xn = xe_ref[H:H + tm, :]
    n_chunks = N_IN // tn
    pieces = tm // sub
    done = 0
    for j in range(1, n_chunks):
        y = _dot(xn, w_ref[:, j * tn:(j + 1) * tn])
        jo = j - 1 - (j > GATE_CHUNK)
        if j == GATE_CHUNK:
            gate_ref[...] = y
        elif j == ROPE_CHUNK:
            W = 2 * DA_HD
            for h in range(tn // W):
                r = _rope(y[:, h * W:(h + 1) * W], rc_ref[...], rsu_ref[...], rsd_ref[...])
                if h < DA_HEADS:
                    r = r * ATTN_Q_SCALE
                o_ref[:, jo * tn + h * W:jo * tn + (h + 1) * W] = r.astype(o_ref.dtype)
        else:
            o_ref[:, jo * tn:(jo + 1) * tn] = y.astype(o_ref.dtype)
        upto = (pieces * j) // (n_chunks - 1)
        for r in range(done, upto):
            conv_rows(r)
        done = upto


def _inproj(x, g, w, dw_w, dw_b, ln_g, ln_b, rope, tm=512, sub=16):
    B, S, D = x.shape
    N = w.shape[1]
    H = CONV_HALO
    nh = tm // H
    last = S // H - 1
    n_out = N - 2 * GATE_CHUNK_W

    def const(shape):
        return pl.BlockSpec(shape, lambda b, i: (0, 0))

    def tok(width):
        return pl.BlockSpec((None, tm, width), lambda b, i: (b, i, 0))

    rope_rows = pl.BlockSpec((tm, 2 * DA_HD), lambda b, i: (i, 0))

    return pl.pallas_call(
        functools.partial(_inproj_kernel, tm=tm, sub=sub),
        grid=(B, S // tm),
        in_specs=[
            pl.BlockSpec((None, H, D), lambda b, i: (b, jnp.maximum(i * nh - 1, 0), 0)),
            tok(D),
            pl.BlockSpec((None, H, D), lambda b, i: (b, jnp.minimum((i + 1) * nh, last), 0)),
            const((1, D)),
            pl.BlockSpec((D, N), lambda b, i: (0, 0), pipeline_mode=pl.Buffered(1)),
            const((CONV_K, CONV_W)), const((1, CONV_W)), const((1, CONV_W)), const((1, CONV_W)),
            rope_rows, rope_rows, rope_rows,
        ],
        out_specs=[tok(n_out), tok(GATE_CHUNK_W), tok(CONV_W)],
        out_shape=[jax.ShapeDtypeStruct((B, S, n_out), BF16),
                   jax.ShapeDtypeStruct((B, S, GATE_CHUNK_W), F32),
                   jax.ShapeDtypeStruct((B, S, CONV_W), BF16)],
        scratch_shapes=[pltpu.VMEM((tm + 2 * H, D), BF16),
                        pltpu.VMEM((tm + 2 * H, CONV_W), F32),
                        pltpu.VMEM((HG_SUB, tm + 2 * H - HG_SUB, CONV_W), F32),
                        pltpu.VMEM((CONV_K, HG_SUB, CONV_W), F32)],
        compiler_params=_cparams(("parallel", "parallel")),
        name="inproj_conv",
    )(x, x, x, g, w, dw_w, dw_b, ln_g, ln_b, *rope)


def _attn_kernel(q_ref, k_ref, v_ref, lam_ref, g_ref, o_ref, vx_ref, qq_ref, sa_ref, sb_ref, *, tq, lam_init):
    S = k_ref.shape[0]
    W = 2 * DA_HD
    nt = S // tq
    vx_ref[:, :W] = v_ref[...]
    vx_ref[:, W:] = jnp.where(lax.broadcasted_iota(jnp.int32, (S, W), 1) == 0, 1.0, 0.0).astype(BF16)

    lp = lam_ref[...]
    lam = (jnp.exp(jnp.sum(lp[0:1] * lp[1:2], axis=-1, keepdims=True))
           - jnp.exp(jnp.sum(lp[2:3] * lp[3:4], axis=-1, keepdims=True)) + lam_init)

    def rows_of(t):
        return pl.ds(pl.multiple_of(t * tq, tq), tq)

    def score_chunk(qq, c, s_ref, mx):
        keys = slice(c * ATTN_KEY_CHUNK, (c + 1) * ATTN_KEY_CHUNK)
        s = _dot_nt(qq, k_ref[keys, :])
        s_ref[:, keys] = s
        for i in range(ATTN_KEY_CHUNK // W):
            mx = jnp.maximum(mx, s[:, i * W:(i + 1) * W])
        return mx

    def write_out(t, r):
        inv = 1.0 / r[:, W:W + 1]
        o = r[:tq, :W] * inv[:tq] - (lam * inv[tq:]) * r[tq:, :W]
        o_ref[rows_of(t), :] = (_rms(o, g_ref[...]) * (1.0 - lam_init)).astype(o_ref.dtype)

    def step(t_next, nxt_ref, cur_ref, m_cur, t_prev, r_prev):
        qq = qq_ref[t_next]
        mx = jnp.full((2 * tq, W), -jnp.inf, F32)
        r = jnp.zeros((2 * tq, 2 * W), F32)
        for c in range(S // ATTN_KEY_CHUNK):
            keys = slice(c * ATTN_KEY_CHUNK, (c + 1) * ATTN_KEY_CHUNK)
            mx = score_chunk(qq, c, nxt_ref, mx)
            p = jnp.exp2(cur_ref[:, keys] - m_cur).astype(BF16)
            r = r + _dot(p, vx_ref[keys, :])
            if c == 0:
                write_out(t_prev, r_prev)
        return jnp.max(mx, axis=-1, keepdims=True), r

    q_all = q_ref[...]
    lane_all = lax.broadcasted_iota(jnp.int32, (S, W), 1) < DA_HD
    q_one = jnp.where(lane_all, q_all, jnp.zeros_like(q_all))
    q_two = jnp.where(lane_all, jnp.zeros_like(q_all), q_all)
    for t in range(nt):
        qq_ref[t, :tq, :] = q_one[t * tq:(t + 1) * tq]
        qq_ref[t, tq:, :] = q_two[t * tq:(t + 1) * tq]

    qq0 = qq_ref[0]
    mx0 = jnp.full((2 * tq, W), -jnp.inf, F32)
    for c in range(S // ATTN_KEY_CHUNK):
        mx0 = score_chunk(qq0, c, sa_ref, mx0)
    m0 = jnp.max(mx0, axis=-1, keepdims=True)

    per_trip = min(ATTN_TILES_PER_TRIP, nt)
    bufs = (sa_ref, sb_ref)

    def body(j, carry):
        m_cur, r_prev = carry
        t0 = per_trip * j
        for u in range(per_trip):
            m_cur, r_prev = step(jnp.minimum(t0 + u + 1, nt - 1), bufs[(u + 1) % 2], bufs[u % 2], m_cur,
                                 jnp.maximum(t0 + u - 1, 0), r_prev)
        return m_cur, r_prev

    _, r_last = lax.fori_loop(0, nt // per_trip, body, (m0, jnp.ones((2 * tq, 2 * W), F32)))
    write_out(nt - 1, r_last)


def _rope_tables(S):
    half = ROT_DIM // 2
    inv = 1.0 / (ROPE_THETA ** (jnp.arange(0, ROT_DIM, 2, dtype=F32) / ROT_DIM))
    ang = jnp.arange(S, dtype=F32)[:, None] * inv[None, :]
    cos, sin = jnp.cos(ang), jnp.sin(ang)
    pad = jnp.zeros((S, DA_HD - ROT_DIM), F32)
    zero = jnp.zeros((S, half), F32)
    c = jnp.concatenate([cos, cos, pad + 1.0], axis=-1)
    s_up = jnp.concatenate([-sin, zero, pad], axis=-1)
    s_dn = jnp.concatenate([zero, sin, pad], axis=-1)
    return tuple(jnp.tile(t, (1, 2)) for t in (c, s_up, s_dn))


def _diff_attention(proj, lam_params, subln_g, layer, tq=256):
    B, S, _ = proj.shape
    W = 2 * DA_HD
    lam_init = 0.8 - 0.6 * math.exp(-0.3 * layer)

    def head(col):
        return pl.BlockSpec((None, S, W), lambda b, h: (b, 0, col * (CONV_W // W) + h))

    def const(shape):
        return pl.BlockSpec(shape, lambda b, h: (0, 0))

    return pl.pallas_call(
        functools.partial(_attn_kernel, tq=tq, lam_init=lam_init),
        grid=(B, DA_HEADS),
        in_specs=[head(COL_AQ), head(COL_AK), head(COL_AV), const((4, DA_HD)), const((1, W))],
        out_specs=pl.BlockSpec((None, S, W), lambda b, h: (b, 0, h)),
        out_shape=jax.ShapeDtypeStruct((B, S, DA_HEADS * W), BF16),
        scratch_shapes=[pltpu.VMEM((S, 2 * W), BF16),
                        pltpu.VMEM((S // tq, 2 * tq, W), BF16),
                        pltpu.VMEM((2 * tq, S), F32), pltpu.VMEM((2 * tq, S), F32)],
        compiler_params=_cparams(("parallel", "parallel")),
        name="diff_attention",
    )(proj, proj, proj, lam_params, subln_g)


def _split3(x):
    hi = x.astype(BF16)
    r1 = x - hi.astype(F32)
    mid = r1.astype(BF16)
    lo = (r1 - mid.astype(F32)).astype(BF16)
    return hi, mid, lo


def _boundary_rows(G, b, rev):
    C, K = G.shape
    e = b if rev else b - 1
    if 2 * b >= HG_SUB:
        G3 = G.reshape(C // (2 * b), 2 * b, K)
        return jnp.broadcast_to(G3[:, e:e + 1, :], G3.shape).reshape(C, K)
    assert b == 2
    G3 = G.reshape(C // HG_SUB, HG_SUB, K)
    lo = jnp.broadcast_to(G3[:, e:e + 1, :], G3.shape)
    hi = jnp.broadcast_to(G3[:, 4 + e:5 + e, :], G3.shape)
    row = lax.broadcasted_iota(jnp.int32, G3.shape, 1)
    return jnp.where(row < 4, lo, hi).reshape(C, K)


def _hgrn_masks(rev):
    C = HG_CHUNK
    r_i = lax.broadcasted_iota(jnp.int32, (C, C), 0)
    c_i = lax.broadcasted_iota(jnp.int32, (C, C), 1)
    row_k = lax.broadcasted_iota(jnp.int32, (C, HG_DK), 0)
    tri = jnp.where((c_i >= r_i) if rev else (c_i <= r_i), 1.0, 0.0).astype(BF16)
    levels = []
    b = C // 2
    while b >= 1:
        t_later = (r_i // b) % 2 == (0 if rev else 1)
        sibling = (c_i // b) == (r_i // b) + (1 if rev else -1)
        q_rows = (row_k // b) % 2 == (0 if rev else 1)
        levels.append((b, t_later & sibling, jnp.where(q_rows, 1.0, -1.0)))
        b //= 2
    return tri, levels


def _hgrn_prep(q_raw, f_raw, v_raw, lb, tri):
    forget = lb + (1.0 - lb) * _sigmoid(f_raw)
    decay = jnp.maximum(forget, GATE_FLOOR)
    hi, mid, lo = _split3(jnp.log(decay))
    G2 = (_dot(tri, hi) + _dot(tri, mid) + _dot(tri, lo)) * LOG2E
    return dict(G2=G2, q=_silu(q_raw.astype(F32)), k=1.0 - forget, v=v_raw, decay=decay)


def _hgrn_scores(t, levels, rev):
    C = HG_CHUNK
    out = []
    for h in range(HG_HEADS):
        sl = slice(h * HG_DK, (h + 1) * HG_DK)
        G2, q, k, decay = t["G2"][:, sl], t["q"][:, sl], t["k"][:, sl], t["decay"][:, sl]
        qb, kb = q.astype(BF16), k.astype(BF16)
        a = jnp.zeros((C, C), F32)
        for b, pair_mask, sign in levels:
            if b == 1:
                qt, kt = (q * decay).astype(BF16), kb
            else:
                d = G2 - _boundary_rows(G2, b, rev)
                e = jnp.exp2(d * sign).astype(BF16)
                qt, kt = qb * e, kb * e
            a = jnp.where(pair_mask, _dot_nt(qt, kt), a)
        out.append(a.astype(BF16))
    return out


def _hgrn_finish(t, scores, states, rev):
    C = HG_CHUNK
    outs, new_states = [], []
    for h in range(HG_HEADS):
        sl = slice(h * HG_DK, (h + 1) * HG_DK)
        G2, q, k, v = t["G2"][:, sl], t["q"][:, sl], t["k"][:, sl], t["v"][:, sl]
        G2_end = G2[0:1] if rev else G2[C - 1:C]
        st = states[h]
        inter = _dot_nt((q * jnp.exp2(G2)).astype(BF16), st.astype(BF16))
        k_dec = k * jnp.exp2(G2_end - G2)
        new_states.append(st * jnp.exp2(G2_end) + _dot_tn(v, k_dec.astype(BF16)))
        intra = _dot(scores[h], v)
        diag = jnp.sum(q * k, axis=-1, keepdims=True) * v.astype(F32)
        outs.append(inter + intra + diag)
    return jnp.concatenate(outs, axis=-1), new_states


def _hgrn_kernel(qf_ref, ff_ref, vf_ref, qb_ref, fb_ref, vb_ref, lb_ref, of_ref, ob_ref, stf_ref, stb_ref):
    @pl.when(pl.program_id(1) == 0)
    def _():
        stf_ref[...] = jnp.zeros_like(stf_ref)
        stb_ref[...] = jnp.zeros_like(stb_ref)

    C = HG_CHUNK
    dirs = [dict(rev=False, q=qf_ref, f=ff_ref, v=vf_ref, lb=lb_ref[0:1, :], o=of_ref, st=stf_ref,
                 order=range(HG_STEP_CHUNKS)),
            dict(rev=True, q=qb_ref, f=fb_ref, v=vb_ref, lb=lb_ref[1:2, :], o=ob_ref, st=stb_ref,
                 order=range(HG_STEP_CHUNKS - 1, -1, -1))]
    for d in dirs:
        d["tri"], d["levels"] = _hgrn_masks(d["rev"])
        d["rows"] = [slice(j * C, (j + 1) * C) for j in d["order"]]
        d["chunks"] = [_hgrn_prep(d["q"][r, :], d["f"][r, :], d["v"][r, :], d["lb"], d["tri"])
                       for r in d["rows"]]
    for d in dirs:
        d["scores"] = [_hgrn_scores(t, d["levels"], d["rev"]) for t in d["chunks"]]
    for d in dirs:
        d["states"] = [d["st"][h] for h in range(HG_HEADS)]
    for j in range(HG_STEP_CHUNKS):
        for d in dirs:
            d["o"][d["rows"][j], :], d["states"] = _hgrn_finish(d["chunks"][j], d["scores"][j], d["states"],
                                                                d["rev"])
    for d in dirs:
        for h in range(HG_HEADS):
            d["st"][h] = d["states"][h]


def _hgrn(proj, gates, lb):
    B, S, _ = proj.shape
    C = HG_CHUNK * HG_STEP_CHUNKS
    n = S // C
    W = HG_HEADS * HG_DK

    def fwd(col):
        return pl.BlockSpec((None, C, W), lambda b, c: (b, c, col))

    def bwd(col):
        return pl.BlockSpec((None, C, W), lambda b, c: (b, n - 1 - c, col))

    return pl.pallas_call(
        _hgrn_kernel,
        grid=(B, n),
        in_specs=[fwd(COL_HQ), fwd(0), fwd(COL_HI), bwd(COL_HQ), bwd(1), bwd(COL_HI),
                  pl.BlockSpec((2, W), lambda b, c: (0, 0))],
        out_specs=[pl.BlockSpec((None, C, W), lambda b, c: (b, c, 0)),
                   pl.BlockSpec((None, C, W), lambda b, c: (b, n - 1 - c, 0))],
        out_shape=[jax.ShapeDtypeStruct((B, S, W), F32), jax.ShapeDtypeStruct((B, S, W), F32)],
        scratch_shapes=[pltpu.VMEM((HG_HEADS, HG_DK, HG_DK), F32), pltpu.VMEM((HG_HEADS, HG_DK, HG_DK), F32)],
        compiler_params=_cparams(("parallel", "arbitrary")),
        name="hgrn2",
    )(proj, gates, proj, proj, gates, proj, lb)


def _memkv_kernel(m_ref, g_ref, w_ref, o_ref):
    o_ref[...] = _dot(_rms(m_ref[...], g_ref[...]).astype(BF16), w_ref[...]).astype(o_ref.dtype)


def _mem_kv(mem, g, w):
    B, M, D = mem.shape
    N = w.shape[1]
    return pl.pallas_call(
        _memkv_kernel,
        grid=(B,),
        in_specs=[pl.BlockSpec((None, M, D), lambda b: (b, 0, 0)),
                  pl.BlockSpec((1, D), lambda b: (0, 0)),
                  pl.BlockSpec((D, N), lambda b: (0, 0))],
        out_specs=pl.BlockSpec((None, M, N), lambda b: (b, 0, 0)),
        out_shape=jax.ShapeDtypeStruct((B, M, N), BF16),
        compiler_params=_cparams(("parallel",)),
        name="mem_kv",
    )(mem, g, w)


def _merge_kernel(x_ref, c_ref, a_ref, of_ref, ob_ref, hg_ref, g0_ref, g1_ref, g2_ref, kv_ref,
                  wc_ref, wa_ref, wh_ref, wo_ref, wq_ref, wco_ref, hgn_ref, ncg_ref, o_ref, *, tm, parts):
    rows = [slice(r * (tm // parts), (r + 1) * (tm // parts)) for r in range(parts)]

    def hgrn_out(r):
        o_sum = of_ref[r, :] + ob_ref[r, :]
        gate = _silu(hg_ref[r, :].astype(F32))
        heads = []
        for h in range(HG_HEADS):
            sl = slice(h * HG_DK, (h + 1) * HG_DK)
            heads.append(_rms(o_sum[:, sl], hgn_ref[...]) * gate[:, sl])
        return jnp.concatenate(heads, axis=-1).astype(BF16)

    def gates(r):
        return [_sigmoid(g_ref[r, :].astype(F32)) for g_ref in (g0_ref, g1_ref, g2_ref)]

    def branches(r, oh):
        return _dot(c_ref[r, :], wc_ref[...]), _dot(a_ref[r, :], wa_ref[...]), _dot(oh, wh_ref[...])

    def mix(r, ys, gs):
        merged = gs[0] * ys[0] + gs[1] * ys[1] + gs[2] * ys[2]
        return x_ref[r, :] + _dot(merged.astype(BF16), wo_ref[...])

    def query(x1):
        hn = _rms(x1, ncg_ref[...]).astype(BF16)
        return (_dot(hn, wq_ref[...]) * (X_HD ** -0.5 * LOG2E)).astype(BF16)

    def scores(q, h):
        return _dot_nt(q[:, h * X_HD:(h + 1) * X_HD], kv_ref[:, h * X_HD:(h + 1) * X_HD])

    def softmax(s):
        p = jnp.exp2(s - jnp.max(s, axis=-1, keepdims=True))
        return (p * (1.0 / jnp.sum(p, axis=-1, keepdims=True))).astype(BF16)

    def values(p, h):
        return _dot(p, kv_ref[:, D_MODEL + h * X_HD:D_MODEL + (h + 1) * X_HD]).astype(BF16)

    oh = [hgrn_out(r) for r in rows]
    ys, gs = [], []
    for r, t in zip(rows, oh):
        ys.append(branches(r, t))
        gs.append(gates(r))
    x1 = [mix(r, y, g) for r, y, g in zip(rows, ys, gs)]
    q = [query(x) for x in x1]
    s = [[scores(t, h) for h in range(X_HEADS)] for t in q]
    p = [[softmax(t) for t in per_group] for per_group in s]
    o = [jnp.concatenate([values(t, h) for h, t in enumerate(per_group)], axis=-1) for per_group in p]
    for r, x, t in zip(rows, x1, o):
        o_ref[r, :] = x + _dot(t, wco_ref[...])


def _merge_cross(x, proj, c, a, o_f, o_b, kv, wc, wa, wh, wo, wq, wco, hgn, ncg, tm=512, parts=2):
    B, S, D = x.shape
    M = kv.shape[1]
    W = CONV_W

    def tok(width, col=0):
        return pl.BlockSpec((None, tm, width), lambda b, i: (b, i, col))

    def const(shape):
        return pl.BlockSpec(shape, lambda b, i: (0, 0), pipeline_mode=pl.Buffered(1))

    return pl.pallas_call(
        functools.partial(_merge_kernel, tm=tm, parts=parts),
        grid=(B, S // tm),
        in_specs=[tok(D), tok(W), tok(W), tok(W), tok(W), tok(W, COL_HG),
                  tok(D, COL_GATE_1024), tok(D, COL_GATE_1024 + 1), tok(D, COL_GATE_1024 + 2),
                  pl.BlockSpec((None, M, 2 * D), lambda b, i: (b, 0, 0)),
                  const((W, D)), const((W, D)), const((W, D)), const((D, D)), const((D, D)), const((D, D)),
                  const((1, HG_DK)), const((1, D))],
        out_specs=tok(D),
        out_shape=jax.ShapeDtypeStruct((B, S, D), F32),
        compiler_params=_cparams(("parallel", "parallel")),
        name="merge_cross",
    )(x, c, a, o_f, o_b, proj, proj, proj, proj, kv, wc, wa, wh, wo, wq, wco, hgn, ncg)


def _ffn_kernel(xp_ref, x_ref, xn_ref, g_ref, wa_ref, wb_ref, dwa_ref, dwb_ref, ba_ref, bb_ref, wd_ref,
                fg_ref, o_ref, hn_ref, acc_ref, *, tm, final):
    i = pl.program_id(1)
    n = pl.num_programs(1)
    f = pl.program_id(2)
    nf = pl.num_programs(2)
    H = FFN_HALO

    @pl.when(f == 0)
    def _():
        g = g_ref[...]
        hn_ref[0:H, :] = jnp.where(i > 0, _rms(xp_ref[...], g), 0.0).astype(BF16)
        hn_ref[H:H + tm, :] = _rms(x_ref[...], g).astype(BF16)
        hn_ref[H + tm:H + tm + H, :] = jnp.where(i < n - 1, _rms(xn_ref[...], g), 0.0).astype(BF16)
        acc_ref[...] = jnp.zeros_like(acc_ref)

    hn = hn_ref[...]
    rows = tm + 2 * H

    def conv(u, dw_ref, b_ref):
        up = pltpu.roll(u, 1, 0)[H:H + tm]
        dn = pltpu.roll(u, rows - 1, 0)[H:H + tm]
        return dw_ref[0:1, :] * up + dw_ref[1:2, :] * u[H:H + tm] + dw_ref[2:3, :] * dn + b_ref[...]

    a = conv(_dot(hn, wa_ref[...]), dwa_ref, ba_ref)
    b = conv(_dot(hn, wb_ref[...]), dwb_ref, bb_ref)
    acc_ref[...] += _dot((_silu(a) * b).astype(BF16), wd_ref[...])

    @pl.when(f == nf - 1)
    def _():
        y = x_ref[...] + acc_ref[...]
        if final:
            y = _rms(y, fg_ref[...])
        o_ref[...] = y


def _conv_ffn(x, g, w_up, dw_w, dw_b, w_down, final_g, final, tm=1024):
    B, S, D = x.shape
    H = FFN_HALO
    nh = tm // H
    last = S // H - 1
    fc = FF_CHUNK
    nf = D_FF // fc

    def const(shape):
        return pl.BlockSpec(shape, lambda b, i, f: (0, 0))

    return pl.pallas_call(
        functools.partial(_ffn_kernel, tm=tm, final=final),
        grid=(B, S // tm, nf),
        in_specs=[
            pl.BlockSpec((None, H, D), lambda b, i, f: (b, jnp.maximum(i * nh - 1, 0), 0)),
            pl.BlockSpec((None, tm, D), lambda b, i, f: (b, i, 0)),
            pl.BlockSpec((None, H, D), lambda b, i, f: (b, jnp.minimum((i + 1) * nh, last), 0)),
            const((1, D)),
            pl.BlockSpec((D, fc), lambda b, i, f: (0, f)),
            pl.BlockSpec((D, fc), lambda b, i, f: (0, nf + f)),
            pl.BlockSpec((3, fc), lambda b, i, f: (0, f)),
            pl.BlockSpec((3, fc), lambda b, i, f: (0, nf + f)),
            pl.BlockSpec((1, fc), lambda b, i, f: (0, f)),
            pl.BlockSpec((1, fc), lambda b, i, f: (0, nf + f)),
            pl.BlockSpec((fc, D), lambda b, i, f: (f, 0)),
            const((1, D)),
        ],
        out_specs=pl.BlockSpec((None, tm, D), lambda b, i, f: (b, i, 0)),
        out_shape=jax.ShapeDtypeStruct((B, S, D), F32),
        scratch_shapes=[pltpu.VMEM((tm + 2 * H, D), BF16), pltpu.VMEM((tm, D), F32)],
        compiler_params=_cparams(("parallel", "parallel", "arbitrary")),
        name="conv_ffn",
    )(x, x, x, g, w_up, w_up, dw_w, dw_w, dw_b, dw_b, w_down, final_g)


def _trunk(x, mem, p, lbs):
    S = x.shape[1]
    rope = _rope_tables(S)
    for l in range(DEPTH):
        proj, gates, c = _inproj(x, p["norm_mix_g"][l], p["w_in"][l], p["conv_dw_w"][l], p["conv_dw_b"][l],
                                 p["conv_ln_g"][l], p["conv_ln_b"][l], rope)
        a = _diff_attention(proj, p["attn_lambda"][l], p["attn_subln_g"][l], l)
        o_f, o_b = _hgrn(proj, gates, lbs[l])
        kv = _mem_kv(mem, p["norm_mem_g"][l], p["w_ckv"][l])
        x = _merge_cross(x, proj, c, a, o_f, o_b, kv, p["w_conv_out"][l], p["w_attn_out"][l],
                         p["w_hg_out"][l], p["w_o"][l], p["w_cq"][l], p["w_co"][l],
                         p["hg_norm_g"][l], p["norm_cross_g"][l])
        x = _conv_ffn(x, p["norm_ffn_g"][l], p["w_up"][l], p["ffn_dw_w"][l], p["ffn_dw_b"][l],
                      p["w_down"][l], p["final_norm_g"], final=(l == DEPTH - 1))
    return x


def _prepare(norm_mix_g, w_in, conv_dw_w, conv_dw_b, conv_ln_g, conv_ln_b, w_conv_out, attn_lambda,
             attn_subln_g, w_attn_out, hg_lb_param, hg_norm_g, w_hg_out, w_o, norm_cross_g, norm_mem_g,
             w_cq, w_ckv, w_co, norm_ffn_g, w_up, ffn_dw_w, ffn_dw_b, w_down, final_norm_g):
    row = lambda t: t.astype(F32)[:, None, :]
    p = {
        "norm_mix_g": row(norm_mix_g), "w_in": w_in.astype(BF16),
        "conv_dw_w": conv_dw_w.astype(F32), "conv_dw_b": row(conv_dw_b),
        "conv_ln_g": row(conv_ln_g), "conv_ln_b": row(conv_ln_b), "w_conv_out": w_conv_out.astype(BF16),
        "attn_lambda": attn_lambda.astype(F32), "attn_subln_g": row(attn_subln_g),
        "w_attn_out": w_attn_out.astype(BF16),
        "hg_norm_g": row(hg_norm_g), "w_hg_out": w_hg_out.astype(BF16), "w_o": w_o.astype(BF16),
        "norm_cross_g": row(norm_cross_g), "norm_mem_g": row(norm_mem_g),
        "w_cq": w_cq.astype(BF16), "w_ckv": w_ckv.astype(BF16), "w_co": w_co.astype(BF16),
        "norm_ffn_g": row(norm_ffn_g), "w_up": w_up.astype(BF16),
        "ffn_dw_w": ffn_dw_w.astype(F32), "ffn_dw_b": row(ffn_dw_b), "w_down": w_down.astype(BF16),
        "final_norm_g": final_norm_g.astype(F32)[None, :],
    }
    lp = jax.nn.softmax(hg_lb_param.astype(F32), axis=0)
    lbs = jnp.cumsum(lp, axis=0) - lp[0:1]
    return p, lbs


def kernel(x_prompt, x_sample, mem_prompt, mem_sample, norm_mix_g, w_in, conv_dw_w, conv_dw_b, conv_ln_g, conv_ln_b, w_conv_out, attn_lambda, attn_subln_g, w_attn_out, hg_lb_param, hg_norm_g, w_hg_out, w_o, norm_cross_g, norm_mem_g, w_cq, w_ckv, w_co, norm_ffn_g, w_up, ffn_dw_w, ffn_dw_b, w_down, final_norm_g):
    p, lbs = _prepare(norm_mix_g, w_in, conv_dw_w, conv_dw_b, conv_ln_g, conv_ln_b, w_conv_out, attn_lambda,
                      attn_subln_g, w_attn_out, hg_lb_param, hg_norm_g, w_hg_out, w_o, norm_cross_g,
                      norm_mem_g, w_cq, w_ckv, w_co, norm_ffn_g, w_up, ffn_dw_w, ffn_dw_b, w_down,
                      final_norm_g)
    y_prompt = _trunk(x_prompt, mem_prompt, p, lbs)
    y_sample = _trunk(x_sample, mem_sample, p, lbs)
    return (y_prompt, y_sample)
```

```python
import functools
import math

import jax
import jax.numpy as jnp
from jax import lax
from jax.experimental import pallas as pl
from jax.experimental.pallas import tpu as pltpu

F32 = jnp.float32
BF16 = jnp.bfloat16

D_MODEL = 1024
DEPTH = 4
CONV_W = 512
CONV_K = 31
CONV_HALO = 16
DA_HEADS = 4
DA_HD = 64
ROT_DIM = 16
ROPE_THETA = 500000.0
ATTN_KEY_CHUNK = 512
ATTN_TILES_PER_TRIP = 8
HG_HEADS = 4
HG_DK = 128
HG_CHUNK = 64
HG_STEP_CHUNKS = 8
HG_SUB = 8
GATE_FLOOR = 1e-30
X_HEADS = 4
X_HD = 256
D_FF = 2816
FF_CHUNK = 1408
FFN_HALO = 8
N_IN = 8192
EPS = 1e-6

GATE_CHUNK_W = 1024
GATE_CHUNK = 3
ROPE_CHUNK = 1
COL_AQ, COL_AK, COL_AV, COL_HQ, COL_HI, COL_HG = range(6)
COL_GATE_1024 = 3
LOG2E = 1.4426950408889634
ATTN_Q_SCALE = DA_HD ** -0.5 * LOG2E

VMEM_LIMIT = 56 * 1024 * 1024


def _cparams(sem):
    return pltpu.CompilerParams(dimension_semantics=sem, vmem_limit_bytes=VMEM_LIMIT)


def _rms(x, g):
    ms = jnp.mean(x * x, axis=-1, keepdims=True)
    return x * lax.rsqrt(ms + EPS) * g


def _sigmoid(x):
    return 1.0 / (1.0 + jnp.exp2(x * -LOG2E))


def _silu(x):
    return x * _sigmoid(x)


def _runtime_zero(v):
    bits = pltpu.bitcast(v, jnp.uint32)
    return ((bits >> 16) >> 16).astype(F32)


def _dot(a, b):
    return jnp.dot(a, b, preferred_element_type=F32)


def _dot_nt(a, b):
    return lax.dot_general(a, b, (((1,), (1,)), ((), ())), preferred_element_type=F32)


def _dot_tn(a, b):
    return lax.dot_general(a, b, (((0,), (0,)), ((), ())), preferred_element_type=F32)


def _rope(x, c, s_up, s_dn):
    n = x.shape[-1]
    half = ROT_DIM // 2
    return x * c + pltpu.roll(x, n - half, 1) * s_up + pltpu.roll(x, half, 1) * s_dn


def _inproj_kernel(xp_ref, x_ref, xn_ref, g_ref, w_ref, cw_ref, cb_ref, lg_ref, lb_ref, rc_ref, rsu_ref, rsd_ref,
                   o_ref, gate_ref, c_ref, xe_ref, ext_ref, sh_ref, cwb_ref, *, tm, sub):
    i = pl.program_id(1)
    n = pl.num_programs(1)
    H = CONV_HALO
    SUBL = HG_SUB
    tn = GATE_CHUNK_W
    g = g_ref[...]
    xe_ref[0:H, :] = jnp.where(i > 0, _rms(xp_ref[...], g), 0.0).astype(BF16)
    xe_ref[H:H + tm, :] = _rms(x_ref[...], g).astype(BF16)
    xe_ref[H + tm:H + tm + H, :] = jnp.where(i < n - 1, _rms(xn_ref[...], g), 0.0).astype(BF16)

    y = _dot(xe_ref[...], w_ref[:, 0:tn])
    ext_ref[...] = y[:, :CONV_W] * _sigmoid(y[:, CONV_W:])
    rows_sh = sh_ref.shape[1]
    for r in range(SUBL):
        sh_ref[r] = ext_ref[pl.ds(r, rows_sh), :]

    for j in range(CONV_K):
        cwb_ref[j] = jnp.broadcast_to(cw_ref[j:j + 1, :], (SUBL, CONV_W))

    def conv_rows(r, anchor):
        off = H - CONV_K // 2
        acc = jnp.zeros((sub // SUBL, SUBL, CONV_W), F32) + anchor[None]
        for j in range(CONV_K):
            start = j + off
            taps = sh_ref[start % SUBL, pl.ds(r * sub + start - start % SUBL, sub), :]
            acc = acc + cwb_ref[j][None] * taps.reshape(sub // SUBL, SUBL, CONV_W)
        nxt = _runtime_zero(acc[0])
        acc = acc.reshape(sub, CONV_W) + cb_ref[...]
        mu = jnp.mean(acc, axis=-1, keepdims=True)
        d = acc - mu
        var = jnp.mean(d * d, axis=-1, keepdims=True)
        z = d * lax.rsqrt(var + EPS) * lg_ref[...] + lb_ref[...]
        c_ref[r * sub:(r + 1) * sub, :] = _silu(z).astype(c_ref.dtype)
        return nxt

    xn = xe_ref[H:H + tm, :]
    n_chunks = N_IN // tn
    pieces = tm // sub
    done = 0
    for j in range(1, n_chunks):
        y = _dot(xn, w_ref[:, j * tn:(j + 1) * tn])
        jo = j - 1 - (j > GATE_CHUNK)
        if j == GATE_CHUNK:
            gate_ref[...] = y
        elif j == ROPE_CHUNK:
            W = 2 * DA_HD
            for h in range(tn // W):
                r = _rope(y[:, h * W:(h + 1) * W], rc_ref[...], rsu_ref[...], rsd_ref[...])
                if h < DA_HEADS:
                    r = r * ATTN_Q_SCALE
                o_ref[:, jo * tn + h * W:jo * tn + (h + 1) * W] = r.astype(o_ref.dtype)
        else:
            o_ref[:, jo * tn:(jo + 1) * tn] = y.astype(o_ref.dtype)
        anchor = _runtime_zero(y[tm - SUBL:tm, tn - CONV_W:tn])
        upto = min(pieces, (pieces * j) // (n_chunks - 2))
        for r in range(done, upto):
            anchor = conv_rows(r, anchor)
        done = upto


def _inproj(x, g, w, dw_w, dw_b, ln_g, ln_b, rope, tm=512, sub=16):
    B, S, D = x.shape
    N = w.shape[1]
    H = CONV_HALO
    nh = tm // H
    last = S // H - 1
    n_out = N - 2 * GATE_CHUNK_W

    def const(shape):
        return pl.BlockSpec(shape, lambda b, i: (0, 0))

    def tok(width):
        return pl.BlockSpec((None, tm, width), lambda b, i: (b, i, 0))

    rope_rows = pl.BlockSpec((tm, 2 * DA_HD), lambda b, i: (i, 0))

    return pl.pallas_call(
        functools.partial(_inproj_kernel, tm=tm, sub=sub),
        grid=(B, S // tm),
        in_specs=[
            pl.BlockSpec((None, H, D), lambda b, i: (b, jnp.maximum(i * nh - 1, 0), 0)),
            tok(D),
            pl.BlockSpec((None, H, D), lambda b, i: (b, jnp.minimum((i + 1) * nh, last), 0)),
            const((1, D)),
            pl.BlockSpec((D, N), lambda b, i: (0, 0), pipeline_mode=pl.Buffered(1)),
            const((CONV_K, CONV_W)), const((1, CONV_W)), const((1, CONV_W)), const((1, CONV_W)),
            rope_rows, rope_rows, rope_rows,
        ],
        out_specs=[tok(n_out), tok(GATE_CHUNK_W), tok(CONV_W)],
        out_shape=[jax.ShapeDtypeStruct((B, S, n_out), BF16),
                   jax.ShapeDtypeStruct((B, S, GATE_CHUNK_W), F32),
                   jax.ShapeDtypeStruct((B, S, CONV_W), BF16)],
        scratch_shapes=[pltpu.VMEM((tm + 2 * H, D), BF16),
                        pltpu.VMEM((tm + 2 * H, CONV_W), F32),
                        pltpu.VMEM((HG_SUB, tm + 2 * H - HG_SUB, CONV_W), F32),
                        pltpu.VMEM((CONV_K, HG_SUB, CONV_W), F32)],
        compiler_params=_cparams(("parallel", "parallel")),
        name="inproj_conv",
    )(x, x, x, g, w, dw_w, dw_b, ln_g, ln_b, *rope)


def _attn_kernel(q_ref, k_ref, v_ref, lam_ref, g_ref, o_ref, vx_ref, qq_ref, sa_ref, sb_ref, *, tq, lam_init):
    S = k_ref.shape[0]
    W = 2 * DA_HD
    nt = S // tq
    vx_ref[:, :W] = v_ref[...]
    vx_ref[:, W:] = jnp.where(lax.broadcasted_iota(jnp.int32, (S, W), 1) == 0, 1.0, 0.0).astype(BF16)

    lp = lam_ref[...]
    lam = (jnp.exp(jnp.sum(lp[0:1] * lp[1:2], axis=-1, keepdims=True))
           - jnp.exp(jnp.sum(lp[2:3] * lp[3:4], axis=-1, keepdims=True)) + lam_init)

    def rows_of(t):
        return pl.ds(pl.multiple_of(t * tq, tq), tq)

    def score_chunk(qq, c, s_ref, mx):
        keys = slice(c * ATTN_KEY_CHUNK, (c + 1) * ATTN_KEY_CHUNK)
        s = _dot_nt(qq, k_ref[keys, :])
        s_ref[:, keys] = s
        for i in range(ATTN_KEY_CHUNK // W):
            mx = jnp.maximum(mx, s[:, i * W:(i + 1) * W])
        return mx

    def write_out(t, r):
        inv = 1.0 / r[:, W:W + 1]
        o = r[:tq, :W] * inv[:tq] - (lam * inv[tq:]) * r[tq:, :W]
        o_ref[rows_of(t), :] = (_rms(o, g_ref[...]) * (1.0 - lam_init)).astype(o_ref.dtype)

    def step(t_next, nxt_ref, cur_ref, m_cur, t_prev, r_prev):
        qq = qq_ref[t_next]
        mx = jnp.full((2 * tq, W), -jnp.inf, F32)
        r = jnp.zeros((2 * tq, 2 * W), F32)
        for c in range(S // ATTN_KEY_CHUNK):
            keys = slice(c * ATTN_KEY_CHUNK, (c + 1) * ATTN_KEY_CHUNK)
            mx = score_chunk(qq, c, nxt_ref, mx)
            p = jnp.exp2(cur_ref[:, keys] - m_cur).astype(BF16)
            r = r + _dot(p, vx_ref[keys, :])
            if c == 0:
                write_out(t_prev, r_prev)
        return jnp.max(mx, axis=-1, keepdims=True), r

    q_all = q_ref[...]
    lane_all = lax.broadcasted_iota(jnp.int32, (S, W), 1) < DA_HD
    q_one = jnp.where(lane_all, q_all, jnp.zeros_like(q_all))
    q_two = jnp.where(lane_all, jnp.zeros_like(q_all), q_all)
    for t in range(nt):
        qq_ref[t, :tq, :] = q_one[t * tq:(t + 1) * tq]
        qq_ref[t, tq:, :] = q_two[t * tq:(t + 1) * tq]

    qq0 = qq_ref[0]
    mx0 = jnp.full((2 * tq, W), -jnp.inf, F32)
    for c in range(S // ATTN_KEY_CHUNK):
        mx0 = score_chunk(qq0, c, sa_ref, mx0)
    m0 = jnp.max(mx0, axis=-1, keepdims=True)

    per_trip = min(ATTN_TILES_PER_TRIP, nt)
    bufs = (sa_ref, sb_ref)

    def body(j, carry):
        m_cur, r_prev = carry
        t0 = per_trip * j
        for u in range(per_trip):
            m_cur, r_prev = step(jnp.minimum(t0 + u + 1, nt - 1), bufs[(u + 1) % 2], bufs[u % 2], m_cur,
                                 jnp.maximum(t0 + u - 1, 0), r_prev)
        return m_cur, r_prev

    _, r_last = lax.fori_loop(0, nt // per_trip, body, (m0, jnp.ones((2 * tq, 2 * W), F32)))
    write_out(nt - 1, r_last)


def _rope_tables(S):
    half = ROT_DIM // 2
    inv = 1.0 / (ROPE_THETA ** (jnp.arange(0, ROT_DIM, 2, dtype=F32) / ROT_DIM))
    ang = jnp.arange(S, dtype=F32)[:, None] * inv[None, :]
    cos, sin = jnp.cos(ang), jnp.sin(ang)
    pad = jnp.zeros((S, DA_HD - ROT_DIM), F32)
    zero = jnp.zeros((S, half), F32)
    c = jnp.concatenate([cos, cos, pad + 1.0], axis=-1)
    s_up = jnp.concatenate([-sin, zero, pad], axis=-1)
    s_dn = jnp.concatenate([zero, sin, pad], axis=-1)
    return tuple(jnp.tile(t, (1, 2)) for t in (c, s_up, s_dn))


def _diff_attention(proj, lam_params, subln_g, layer, tq=256):
    B, S, _ = proj.shape
    W = 2 * DA_HD
    lam_init = 0.8 - 0.6 * math.exp(-0.3 * layer)

    def head(col):
        return pl.BlockSpec((None, S, W), lambda b, h: (b, 0, col * (CONV_W // W) + h))

    def const(shape):
        return pl.BlockSpec(shape, lambda b, h: (0, 0))

    return pl.pallas_call(
        functools.partial(_attn_kernel, tq=tq, lam_init=lam_init),
        grid=(B, DA_HEADS),
        in_specs=[head(COL_AQ), head(COL_AK), head(COL_AV), const((4, DA_HD)), const((1, W))],
        out_specs=pl.BlockSpec((None, S, W), lambda b, h: (b, 0, h)),
        out_shape=jax.ShapeDtypeStruct((B, S, DA_HEADS * W), BF16),
        scratch_shapes=[pltpu.VMEM((S, 2 * W), BF16),
                        pltpu.VMEM((S // tq, 2 * tq, W), BF16),
                        pltpu.VMEM((2 * tq, S), F32), pltpu.VMEM((2 * tq, S), F32)],
        compiler_params=_cparams(("parallel", "parallel")),
        name="diff_attention",
    )(proj, proj, proj, lam_params, subln_g)


def _split3(x):
    hi = x.astype(BF16)
    r1 = x - hi.astype(F32)
    mid = r1.astype(BF16)
    lo = (r1 - mid.astype(F32)).astype(BF16)
    return hi, mid, lo


def _boundary_rows(G, b, rev):
    C, K = G.shape
    e = b if rev else b - 1
    if 2 * b >= HG_SUB:
        G3 = G.reshape(C // (2 * b), 2 * b, K)
        return jnp.broadcast_to(G3[:, e:e + 1, :], G3.shape).reshape(C, K)
    assert b == 2
    G3 = G.reshape(C // HG_SUB, HG_SUB, K)
    lo = jnp.broadcast_to(G3[:, e:e + 1, :], G3.shape)
    hi = jnp.broadcast_to(G3[:, 4 + e:5 + e, :], G3.shape)
    row = lax.broadcasted_iota(jnp.int32, G3.shape, 1)
    return jnp.where(row < 4, lo, hi).reshape(C, K)


def _hgrn_masks(rev):
    C = HG_CHUNK
    r_i = lax.broadcasted_iota(jnp.int32, (C, C), 0)
    c_i = lax.broadcasted_iota(jnp.int32, (C, C), 1)
    row_k = lax.broadcasted_iota(jnp.int32, (C, HG_DK), 0)
    tri = jnp.where((c_i >= r_i) if rev else (c_i <= r_i), 1.0, 0.0).astype(BF16)
    levels = []
    b = C // 2
    while b >= 1:
        t_later = (r_i // b) % 2 == (0 if rev else 1)
        sibling = (c_i // b) == (r_i // b) + (1 if rev else -1)
        q_rows = (row_k // b) % 2 == (0 if rev else 1)
        levels.append((b, t_later & sibling, jnp.where(q_rows, 1.0, -1.0)))
        b //= 2
    return tri, levels


def _hgrn_prep(q_raw, f_raw, v_raw, lb, tri):
    forget = lb + (1.0 - lb) * _sigmoid(f_raw)
    decay = jnp.maximum(forget, GATE_FLOOR)
    hi, mid, lo = _split3(jnp.log(decay))
    G2 = (_dot(tri, hi) + _dot(tri, mid) + _dot(tri, lo)) * LOG2E
    return dict(G2=G2, q=_silu(q_raw.astype(F32)), k=1.0 - forget, v=v_raw, decay=decay)


def _hgrn_scores(t, levels, rev):
    C = HG_CHUNK
    out = []
    for h in range(HG_HEADS):
        sl = slice(h * HG_DK, (h + 1) * HG_DK)
        G2, q, k, decay = t["G2"][:, sl], t["q"][:, sl], t["k"][:, sl], t["decay"][:, sl]
        qb, kb = q.astype(BF16), k.astype(BF16)
        a = jnp.zeros((C, C), F32)
        for b, pair_mask, sign in levels:
            if b == 1:
                qt, kt = (q * decay).astype(BF16), kb
            else:
                d = G2 - _boundary_rows(G2, b, rev)
                e = jnp.exp2(d * sign).astype(BF16)
                qt, kt = qb * e, kb * e
            a = jnp.where(pair_mask, _dot_nt(qt, kt), a)
        out.append(a.astype(BF16))
    return out


def _hgrn_finish(t, scores, states, rev):
    C = HG_CHUNK
    outs, new_states = [], []
    for h in range(HG_HEADS):
        sl = slice(h * HG_DK, (h + 1) * HG_DK)
        G2, q, k, v = t["G2"][:, sl], t["q"][:, sl], t["k"][:, sl], t["v"][:, sl]
        G2_end = G2[0:1] if rev else G2[C - 1:C]
        st = states[h]
        inter = _dot_nt((q * jnp.exp2(G2)).astype(BF16), st.astype(BF16))
        k_dec = k * jnp.exp2(G2_end - G2)
        new_states.append(st * jnp.exp2(G2_end) + _dot_tn(v, k_dec.astype(BF16)))
        intra = _dot(scores[h], v)
        diag = jnp.sum(q * k, axis=-1, keepdims=True) * v.astype(F32)
        outs.append(inter + intra + diag)
    return jnp.concatenate(outs, axis=-1), new_states


def _hgrn_kernel(qf_ref, ff_ref, vf_ref, qb_ref, fb_ref, vb_ref, lb_ref, of_ref, ob_ref, stf_ref, stb_ref):
    @pl.when(pl.program_id(1) == 0)
    def _():
        stf_ref[...] = jnp.zeros_like(stf_ref)
        stb_ref[...] = jnp.zeros_like(stb_ref)

    C = HG_CHUNK
    dirs = [dict(rev=False, q=qf_ref, f=ff_ref, v=vf_ref, lb=lb_ref[0:1, :], o=of_ref, st=stf_ref,
                 order=range(HG_STEP_CHUNKS)),
            dict(rev=True, q=qb_ref, f=fb_ref, v=vb_ref, lb=lb_ref[1:2, :], o=ob_ref, st=stb_ref,
                 order=range(HG_STEP_CHUNKS - 1, -1, -1))]
    for d in dirs:
        d["tri"], d["levels"] = _hgrn_masks(d["rev"])
        d["rows"] = [slice(j * C, (j + 1) * C) for j in d["order"]]
        d["chunks"] = [_hgrn_prep(d["q"][r, :], d["f"][r, :], d["v"][r, :], d["lb"], d["tri"])
                       for r in d["rows"]]
    for d in dirs:
        d["scores"] = [_hgrn_scores(t, d["levels"], d["rev"]) for t in d["chunks"]]
    for d in dirs:
        d["states"] = [d["st"][h] for h in range(HG_HEADS)]
    for j in range(HG_STEP_CHUNKS):
        for d in dirs:
            d["o"][d["rows"][j], :], d["states"] = _hgrn_finish(d["chunks"][j], d["scores"][j], d["states"],
                                                                d["rev"])
    for d in dirs:
        for h in range(HG_HEADS):
            d["st"][h] = d["states"][h]


def _hgrn(proj, gates, lb):
    B, S, _ = proj.shape
    C = HG_CHUNK * HG_STEP_CHUNKS
    n = S // C
    W = HG_HEADS * HG_DK

    def fwd(col):
        return pl.BlockSpec((None, C, W), lambda b, c: (b, c, col))

    def bwd(col):
        return pl.BlockSpec((None, C, W), lambda b, c: (b, n - 1 - c, col))

    return pl.pallas_call(
        _hgrn_kernel,
        grid=(B, n),
        in_specs=[fwd(COL_HQ), fwd(0), fwd(COL_HI), bwd(COL_HQ), bwd(1), bwd(COL_HI),
                  pl.BlockSpec((2, W), lambda b, c: (0, 0))],
        out_specs=[pl.BlockSpec((None, C, W), lambda b, c: (b, c, 0)),
                   pl.BlockSpec((None, C, W), lambda b, c: (b, n - 1 - c, 0))],
        out_shape=[jax.ShapeDtypeStruct((B, S, W), F32), jax.ShapeDtypeStruct((B, S, W), F32)],
        scratch_shapes=[pltpu.VMEM((HG_HEADS, HG_DK, HG_DK), F32), pltpu.VMEM((HG_HEADS, HG_DK, HG_DK), F32)],
        compiler_params=_cparams(("parallel", "arbitrary")),
        name="hgrn2",
    )(proj, gates, proj, proj, gates, proj, lb)


def _memkv_kernel(m_ref, g_ref, w_ref, o_ref):
    o_ref[...] = _dot(_rms(m_ref[...], g_ref[...]).astype(BF16), w_ref[...]).astype(o_ref.dtype)


def _mem_kv(mem, g, w):
    B, M, D = mem.shape
    N = w.shape[1]
    return pl.pallas_call(
        _memkv_kernel,
        grid=(B,),
        in_specs=[pl.BlockSpec((None, M, D), lambda b: (b, 0, 0)),
                  pl.BlockSpec((1, D), lambda b: (0, 0)),
                  pl.BlockSpec((D, N), lambda b: (0, 0))],
        out_specs=pl.BlockSpec((None, M, N), lambda b: (b, 0, 0)),
        out_shape=jax.ShapeDtypeStruct((B, M, N), BF16),
        compiler_params=_cparams(("parallel",)),
        name="mem_kv",
    )(mem, g, w)


def _merge_kernel(x_ref, c_ref, a_ref, of_ref, ob_ref, hg_ref, g0_ref, g1_ref, g2_ref, kv_ref,
                  wc_ref, wa_ref, wh_ref, wo_ref, wq_ref, wco_ref, hgn_ref, ncg_ref, o_ref, *, tm, parts):
    rows = [slice(r * (tm // parts), (r + 1) * (tm // parts)) for r in range(parts)]

    def hgrn_out(r):
        o_sum = of_ref[r, :] + ob_ref[r, :]
        gate = _silu(hg_ref[r, :].astype(F32))
        heads = []
        for h in range(HG_HEADS):
            sl = slice(h * HG_DK, (h + 1) * HG_DK)
            heads.append(_rms(o_sum[:, sl], hgn_ref[...]) * gate[:, sl])
        return jnp.concatenate(heads, axis=-1).astype(BF16)

    def gates(r):
        return [_sigmoid(g_ref[r, :].astype(F32)) for g_ref in (g0_ref, g1_ref, g2_ref)]

    def branches(r, oh):
        return _dot(c_ref[r, :], wc_ref[...]), _dot(a_ref[r, :], wa_ref[...]), _dot(oh, wh_ref[...])

    def mix(r, ys, gs):
        merged = gs[0] * ys[0] + gs[1] * ys[1] + gs[2] * ys[2]
        return x_ref[r, :] + _dot(merged.astype(BF16), wo_ref[...])

    def query(x1):
        hn = _rms(x1, ncg_ref[...]).astype(BF16)
        return (_dot(hn, wq_ref[...]) * (X_HD ** -0.5 * LOG2E)).astype(BF16)

    def scores(q, h):
        return _dot_nt(q[:, h * X_HD:(h + 1) * X_HD], kv_ref[:, h * X_HD:(h + 1) * X_HD])

    def softmax(s):
        p = jnp.exp2(s - jnp.max(s, axis=-1, keepdims=True))
        return (p * (1.0 / jnp.sum(p, axis=-1, keepdims=True))).astype(BF16)

    def values(p, h):
        return _dot(p, kv_ref[:, D_MODEL + h * X_HD:D_MODEL + (h + 1) * X_HD]).astype(BF16)

    oh = [hgrn_out(r) for r in rows]
    ys, gs = [], []
    for r, t in zip(rows, oh):
        ys.append(branches(r, t))
        gs.append(gates(r))
    x1 = [mix(r, y, g) for r, y, g in zip(rows, ys, gs)]
    q = [query(x) for x in x1]
    s = [[scores(t, h) for h in range(X_HEADS)] for t in q]
    p = [[softmax(t) for t in per_group] for per_group in s]
    o = [jnp.concatenate([values(t, h) for h, t in enumerate(per_group)], axis=-1) for per_group in p]
    for r, x, t in zip(rows, x1, o):
        o_ref[r, :] = x + _dot(t, wco_ref[...])


def _merge_cross(x, proj, c, a, o_f, o_b, kv, wc, wa, wh, wo, wq, wco, hgn, ncg, tm=512, parts=2):
    B, S, D = x.shape
    M = kv.shape[1]
    W = CONV_W

    def tok(width, col=0):
        return pl.BlockSpec((None, tm, width), lambda b, i: (b, i, col))

    def const(shape):
        return pl.BlockSpec(shape, lambda b, i: (0, 0), pipeline_mode=pl.Buffered(1))

    return pl.pallas_call(
        functools.partial(_merge_kernel, tm=tm, parts=parts),
        grid=(B, S // tm),
        in_specs=[tok(D), tok(W), tok(W), tok(W), tok(W), tok(W, COL_HG),
                  tok(D, COL_GATE_1024), tok(D, COL_GATE_1024 + 1), tok(D, COL_GATE_1024 + 2),
                  pl.BlockSpec((None, M, 2 * D), lambda b, i: (b, 0, 0)),
                  const((W, D)), const((W, D)), const((W, D)), const((D, D)), const((D, D)), const((D, D)),
                  const((1, HG_DK)), const((1, D))],
        out_specs=tok(D),
        out_shape=jax.ShapeDtypeStruct((B, S, D), F32),
        compiler_params=_cparams(("parallel", "parallel")),
        name="merge_cross",
    )(x, c, a, o_f, o_b, proj, proj, proj, proj, kv, wc, wa, wh, wo, wq, wco, hgn, ncg)


def _ffn_kernel(xp_ref, x_ref, xn_ref, g_ref, wa_ref, wb_ref, dwa_ref, dwb_ref, ba_ref, bb_ref, wd_ref,
                fg_ref, o_ref, hn_ref, acc_ref, *, tm, final):
    i = pl.program_id(1)
    n = pl.num_programs(1)
    f = pl.program_id(2)
    nf = pl.num_programs(2)
    H = FFN_HALO

    @pl.when(f == 0)
    def _():
        g = g_ref[...]
        hn_ref[0:H, :] = jnp.where(i > 0, _rms(xp_ref[...], g), 0.0).astype(BF16)
        hn_ref[H:H + tm, :] = _rms(x_ref[...], g).astype(BF16)
        hn_ref[H + tm:H + tm + H, :] = jnp.where(i < n - 1, _rms(xn_ref[...], g), 0.0).astype(BF16)
        acc_ref[...] = jnp.zeros_like(acc_ref)

    hn = hn_ref[...]
    rows = tm + 2 * H

    def conv(u, dw_ref, b_ref):
        up = pltpu.roll(u, 1, 0)[H:H + tm]
        dn = pltpu.roll(u, rows - 1, 0)[H:H + tm]
        return dw_ref[0:1, :] * up + dw_ref[1:2, :] * u[H:H + tm] + dw_ref[2:3, :] * dn + b_ref[...]

    a = conv(_dot(hn, wa_ref[...]), dwa_ref, ba_ref)
    b = conv(_dot(hn, wb_ref[...]), dwb_ref, bb_ref)
    acc_ref[...] += _dot((_silu(a) * b).astype(BF16), wd_ref[...])

    @pl.when(f == nf - 1)
    def _():
        y = x_ref[...] + acc_ref[...]
        if final:
            y = _rms(y, fg_ref[...])
        o_ref[...] = y


def _conv_ffn(x, g, w_up, dw_w, dw_b, w_down, final_g, final, tm=1024):
    B, S, D = x.shape
    H = FFN_HALO
    nh = tm // H
    last = S // H - 1
    fc = FF_CHUNK
    nf = D_FF // fc

    def const(shape):
        return pl.BlockSpec(shape, lambda b, i, f: (0, 0))

    return pl.pallas_call(
        functools.partial(_ffn_kernel, tm=tm, final=final),
        grid=(B, S // tm, nf),
        in_specs=[
            pl.BlockSpec((None, H, D), lambda b, i, f: (b, jnp.maximum(i * nh - 1, 0), 0)),
            pl.BlockSpec((None, tm, D), lambda b, i, f: (b, i, 0)),
            pl.BlockSpec((None, H, D), lambda b, i, f: (b, jnp.minimum((i + 1) * nh, last), 0)),
            const((1, D)),
            pl.BlockSpec((D, fc), lambda b, i, f: (0, f)),
            pl.BlockSpec((D, fc), lambda b, i, f: (0, nf + f)),
            pl.BlockSpec((3, fc), lambda b, i, f: (0, f)),
            pl.BlockSpec((3, fc), lambda b, i, f: (0, nf + f)),
            pl.BlockSpec((1, fc), lambda b, i, f: (0, f)),
            pl.BlockSpec((1, fc), lambda b, i, f: (0, nf + f)),
            pl.BlockSpec((fc, D), lambda b, i, f: (f, 0)),
            const((1, D)),
        ],
        out_specs=pl.BlockSpec((None, tm, D), lambda b, i, f: (b, i, 0)),
        out_shape=jax.ShapeDtypeStruct((B, S, D), F32),
        scratch_shapes=[pltpu.VMEM((tm + 2 * H, D), BF16), pltpu.VMEM((tm, D), F32)],
        compiler_params=_cparams(("parallel", "parallel", "arbitrary")),
        name="conv_ffn",
    )(x, x, x, g, w_up, w_up, dw_w, dw_w, dw_b, dw_b, w_down, final_g)


def _trunk(x, mem, p, lbs):
    S = x.shape[1]
    rope = _rope_tables(S)
    for l in range(DEPTH):
        proj, gates, c = _inproj(x, p["norm_mix_g"][l], p["w_in"][l], p["conv_dw_w"][l], p["conv_dw_b"][l],
                                 p["conv_ln_g"][l], p["conv_ln_b"][l], rope)
        a = _diff_attention(proj, p["attn_lambda"][l], p["attn_subln_g"][l], l)
        o_f, o_b = _hgrn(proj, gates, lbs[l])
        kv = _mem_kv(mem, p["norm_mem_g"][l], p["w_ckv"][l])
        x = _merge_cross(x, proj, c, a, o_f, o_b, kv, p["w_conv_out"][l], p["w_attn_out"][l],
                         p["w_hg_out"][l], p["w_o"][l], p["w_cq"][l], p["w_co"][l],
                         p["hg_norm_g"][l], p["norm_cross_g"][l])
        x = _conv_ffn(x, p["norm_ffn_g"][l], p["w_up"][l], p["ffn_dw_w"][l], p["ffn_dw_b"][l],
                      p["w_down"][l], p["final_norm_g"], final=(l == DEPTH - 1))
    return x


def _prepare(norm_mix_g, w_in, conv_dw_w, conv_dw_b, conv_ln_g, conv_ln_b, w_conv_out, attn_lambda,
             attn_subln_g, w_attn_out, hg_lb_param, hg_norm_g, w_hg_out, w_o, norm_cross_g, norm_mem_g,
             w_cq, w_ckv, w_co, norm_ffn_g, w_up, ffn_dw_w, ffn_dw_b, w_down, final_norm_g):
    row = lambda t: t.astype(F32)[:, None, :]
    p = {
        "norm_mix_g": row(norm_mix_g), "w_in": w_in.astype(BF16),
        "conv_dw_w": conv_dw_w.astype(F32), "conv_dw_b": row(conv_dw_b),
        "conv_ln_g": row(conv_ln_g), "conv_ln_b": row(conv_ln_b), "w_conv_out": w_conv_out.astype(BF16),
        "attn_lambda": attn_lambda.astype(F32), "attn_subln_g": row(attn_subln_g),
        "w_attn_out": w_attn_out.astype(BF16),
        "hg_norm_g": row(hg_norm_g), "w_hg_out": w_hg_out.astype(BF16), "w_o": w_o.astype(BF16),
        "norm_cross_g": row(norm_cross_g), "norm_mem_g": row(norm_mem_g),
        "w_cq": w_cq.astype(BF16), "w_ckv": w_ckv.astype(BF16), "w_co": w_co.astype(BF16),
        "norm_ffn_g": row(norm_ffn_g), "w_up": w_up.astype(BF16),
        "ffn_dw_w": ffn_dw_w.astype(F32), "ffn_dw_b": row(ffn_dw_b), "w_down": w_down.astype(BF16),
        "final_norm_g": final_norm_g.astype(F32)[None, :],
    }
    lp = jax.nn.softmax(hg_lb_param.astype(F32), axis=0)
    lbs = jnp.cumsum(lp, axis=0) - lp[0:1]
    return p, lbs


def kernel(x_prompt, x_sample, mem_prompt, mem_sample, norm_mix_g, w_in, conv_dw_w, conv_dw_b, conv_ln_g, conv_ln_b, w_conv_out, attn_lambda, attn_subln_g, w_attn_out, hg_lb_param, hg_norm_g, w_hg_out, w_o, norm_cross_g, norm_mem_g, w_cq, w_ckv, w_co, norm_ffn_g, w_up, ffn_dw_w, ffn_dw_b, w_down, final_norm_g):
    p, lbs = _prepare(norm_mix_g, w_in, conv_dw_w, conv_dw_b, conv_ln_g, conv_ln_b, w_conv_out, attn_lambda,
                      attn_subln_g, w_attn_out, hg_lb_param, hg_norm_g, w_hg_out, w_o, norm_cross_g,
                      norm_mem_g, w_cq, w_ckv, w_co, norm_ffn_g, w_up, ffn_dw_w, ffn_dw_b, w_down,
                      final_norm_g)
    y_prompt = _trunk(x_prompt, mem_prompt, p, lbs)
    y_sample = _trunk(x_sample, mem_sample, p, lbs)
    return (y_prompt, y_sample)
```
